```python
import math
import jax, jax.numpy as jnp
from jax import lax
import numpy as np

D_MODEL = 4096
BATCH = 1
SEQ = 8192
DEPTH = 2

HEAD_DIM = 128
N_HEADS = D_MODEL // HEAD_DIM
ROPE_DIMS = HEAD_DIM // 4
ROPE_THETA = 500000.0
AXIAL_THETA = 10000.0
GRID_W = 64
Q_BLOCK = 128
A_HEADS = N_HEADS // 2
A_KV_HEADS = A_HEADS // 4
A_GROUP = A_HEADS // A_KV_HEADS
A_WINDOW = 128
B_PAIRS = ((128, 1), (512, 4), (2048, 16))
B_GROUP_HEADS = N_HEADS // 4
B_HEADS = len(B_PAIRS) * B_GROUP_HEADS
C_HEADS = N_HEADS // 4
D_HEADS = N_HEADS // 2
D_KV_HEADS = D_HEADS // 4
D_GROUP = D_HEADS // D_KV_HEADS
PEER_HEADS = 8
PEER_NKEYS = 128
PEER_EXPERTS = PEER_NKEYS * PEER_NKEYS
PEER_QDIM = 256
PEER_HALF = PEER_QDIM // 2
PEER_TOPK = 16
PEER_CHUNK = 128
DEEPNORM_ALPHA = (2 * DEPTH) ** 0.25
DEEPNORM_BETA = (8 * DEPTH) ** -0.25
N_EVEN = (DEPTH + 1) // 2
N_ODD = DEPTH // 2
LN_EPS = 1e-5
RMS_EPS = 1e-6
NEG_INF = -1e30

AB_SIZES = (A_HEADS * HEAD_DIM, A_KV_HEADS * HEAD_DIM, A_KV_HEADS * HEAD_DIM,
            B_HEADS * HEAD_DIM, B_HEADS * HEAD_DIM, B_HEADS * HEAD_DIM)
AB_OUT = (A_HEADS + B_GROUP_HEADS) * HEAD_DIM
CD_SIZES = (C_HEADS * 2 * HEAD_DIM, C_HEADS * 2 * HEAD_DIM, C_HEADS * 2 * HEAD_DIM,
            D_HEADS * HEAD_DIM, D_KV_HEADS * HEAD_DIM, D_KV_HEADS * HEAD_DIM)
CD_OUT = (2 * C_HEADS + D_HEADS) * HEAD_DIM

kernel_name = "hybrid_window_dilated_diff_axial_peer_encoder"


def layer_norm(x, g, b):
    xf = x.astype(jnp.float32)
    mu = xf.mean(-1, keepdims=True)
    var = jnp.square(xf - mu).mean(-1, keepdims=True)
    return ((xf - mu) * lax.rsqrt(var + LN_EPS) * g.astype(jnp.float32) + b.astype(jnp.float32)).astype(x.dtype)


def rms_norm(x, g):
    xf = x.astype(jnp.float32)
    return (xf * lax.rsqrt(jnp.square(xf).mean(-1, keepdims=True) + RMS_EPS) * g.astype(jnp.float32)).astype(x.dtype)


def rope(x, pos, start, n, theta):
    half = n // 2
    inv = theta ** (-jnp.arange(half, dtype=jnp.float32) / half)
    ang = pos.astype(jnp.float32)[:, None] * inv[None, :]
    cos, sin = jnp.cos(ang), jnp.sin(ang)
    xs = x[..., start:start + n].astype(jnp.float32)
    x1, x2 = xs[..., :half], xs[..., half:]
    rot = jnp.concatenate([x1 * cos - x2 * sin, x2 * cos + x1 * sin], -1).astype(x.dtype)
    return jnp.concatenate([x[..., :start], rot, x[..., start + n:]], -1)


def partial_rope(x, pos):
    return rope(x, pos, 0, ROPE_DIMS, ROPE_THETA)


def axial_rope(x, rows, cols):
    half = HEAD_DIM // 2
    return rope(rope(x, rows, 0, half, AXIAL_THETA), cols, half, half, AXIAL_THETA)


def split_cols(h, sizes):
    return jnp.split(h, np.cumsum(sizes)[:-1].tolist(), axis=-1)


def to_heads(t, n, dh):
    b, s, _ = t.shape
    return t.reshape(b, s, n, dh).transpose(0, 2, 1, 3)


def from_heads(o):
    b, h, s, dh = o.shape
    return o.transpose(0, 2, 1, 3).reshape(b, s, h * dh)


def banded_attention(q, k, v, window, block, sink=None):
    L, hd = q.shape[-2], q.shape[-1]
    nb = -(-L // block)
    pad = nb * block - L
    qb = jnp.pad(q, [(0, 0)] * (q.ndim - 2) + [(0, pad), (0, 0)])
    qb = qb.reshape(q.shape[:-2] + (nb, block, hd))

    def windows(a):
        ab = jnp.pad(a, [(0, 0)] * (a.ndim - 2) + [(block, pad + block), (0, 0)])
        ab = ab.reshape(a.shape[:-2] + (nb + 2, block, a.shape[-1]))
        return jnp.concatenate([ab[..., :-2, :, :], ab[..., 1:-1, :, :], ab[..., 2:, :, :]], axis=-2)

    kw, vw = windows(k), windows(v)
    s = jnp.einsum('...gnqd,...nkd->...gnqk', qb, kw,
                   preferred_element_type=jnp.float32) * (hd ** -0.5)
    qpos = jnp.arange(nb)[:, None] * block + jnp.arange(block)[None, :]
    kpos = (jnp.arange(nb)[:, None] - 1) * block + jnp.arange(3 * block)[None, :]
    mask = ((jnp.abs(qpos[:, :, None] - kpos[:, None, :]) <= window)
            & (kpos[:, None, :] >= 0) & (kpos[:, None, :] < L))
    s = jnp.where(mask, s, NEG_INF)
    m = s.max(-1)
    if sink is not None:
        sink_b = sink.astype(jnp.float32).reshape(sink.shape + (1, 1))
        m = jnp.maximum(m, sink_b)
    p = jnp.exp(s - m[..., None])
    denom = p.sum(-1)
    if sink is not None:
        denom = denom + jnp.exp(sink_b - m)
    o = jnp.einsum('...gnqk,...nkd->...gnqd', (p / denom[..., None]).astype(v.dtype), vw)
    o = o.reshape(q.shape[:-2] + (nb * block, hd))[..., :L, :]
    lse = (m + jnp.log(denom)).reshape(q.shape[:-2] + (nb * block,))[..., :L]
    return o, lse


def dilated_attention(q, k, v, dilation, n_side):
    b, h, s, hd = q.shape
    L = s // dilation
    fold = lambda t: t.reshape(b, h, L, dilation, hd).transpose(0, 1, 3, 2, 4)
    o, lse = banded_attention(fold(q)[:, :, :, None], fold(k), fold(v), n_side, n_side)
    o = o[:, :, :, 0].transpose(0, 1, 3, 2, 4).reshape(b, h, s, hd)
    lse = lse[:, :, :, 0].transpose(0, 1, 3, 2).reshape(b, h, s)
    return o, lse


def sweep_query_blocks(fn, qs):
    s = qs[0].shape[-2]
    nb = s // Q_BLOCK

    def split(a):
        return jnp.moveaxis(a.reshape(a.shape[:-2] + (nb, Q_BLOCK, a.shape[-1])), -3, 0)

    out = lax.map(lambda blks: fn(*blks), tuple(split(a) for a in qs))
    out = jnp.moveaxis(out, 0, -3)
    return out.reshape(out.shape[:-3] + (s, out.shape[-1]))


def even_mixer(x, w_in, sink, w_out, pos):
    b, s, _ = x.shape
    qa, ka, va, qb, kb, vb = split_cols(x @ w_in, AB_SIZES)
    qa = partial_rope(to_heads(qa, A_HEADS, HEAD_DIM), pos).reshape(b, A_KV_HEADS, A_GROUP, s, HEAD_DIM)
    ka = partial_rope(to_heads(ka, A_KV_HEADS, HEAD_DIM), pos)
    va = to_heads(va, A_KV_HEADS, HEAD_DIM)
    oa, _ = banded_attention(qa, ka, va, A_WINDOW, A_WINDOW, sink)
    oa = from_heads(oa.reshape(b, A_HEADS, s, HEAD_DIM))
    qb = partial_rope(to_heads(qb, B_HEADS, HEAD_DIM), pos)
    kb = partial_rope(to_heads(kb, B_HEADS, HEAD_DIM), pos)
    vb = to_heads(vb, B_HEADS, HEAD_DIM)
    outs, lses = [], []
    for g, (win, dil) in enumerate(B_PAIRS):
        sl = slice(g * B_GROUP_HEADS, (g + 1) * B_GROUP_HEADS)
        o, l = dilated_attention(qb[:, sl], kb[:, sl], vb[:, sl], dil, win // (2 * dil))
        outs.append(o)
        lses.append(l)
    wts = jax.nn.softmax(jnp.stack(lses), axis=0)
    ob = jnp.sum(wts[..., None] * jnp.stack(outs).astype(jnp.float32), axis=0).astype(x.dtype)
    ob = from_heads(ob)
    return jnp.concatenate([oa, ob], -1) @ w_out


def odd_mixer(x, w_in, lam_q1, lam_k1, lam_q2, lam_k2, subln_g, q_norm_g, k_norm_g, w_out,
              pos, rows, cols, lambda_init):
    b, s, _ = x.shape
    qc, kc, vc, qd, kd, vd = split_cols(x @ w_in, CD_SIZES)
    qc = to_heads(qc, C_HEADS, 2 * HEAD_DIM)
    kc = to_heads(kc, C_HEADS, 2 * HEAD_DIM)
    vc = to_heads(vc, C_HEADS, 2 * HEAD_DIM)
    q1, q2 = partial_rope(qc[..., :HEAD_DIM], pos), partial_rope(qc[..., HEAD_DIM:], pos)
    k1, k2 = partial_rope(kc[..., :HEAD_DIM], pos), partial_rope(kc[..., HEAD_DIM:], pos)
    f32 = jnp.float32
    lam = (jnp.exp(jnp.sum(lam_q1.astype(f32) * lam_k1.astype(f32)))
           - jnp.exp(jnp.sum(lam_q2.astype(f32) * lam_k2.astype(f32))) + lambda_init)
    scale = HEAD_DIM ** -0.5

    def c_block(q1b, q2b):
        s1 = jnp.einsum('bhqd,bhsd->bhqs', q1b, k1, preferred_element_type=f32) * scale
        s2 = jnp.einsum('bhqd,bhsd->bhqs', q2b, k2, preferred_element_type=f32) * scale
        a = jax.nn.softmax(s1, -1) - lam * jax.nn.softmax(s2, -1)
        return jnp.einsum('bhqs,bhsd->bhqd', a.astype(vc.dtype), vc)

    oc = sweep_query_blocks(c_block, (q1, q2))
    oc = from_heads(rms_norm(oc, subln_g) * (1.0 - lambda_init))
    qd = axial_rope(rms_norm(to_heads(qd, D_HEADS, HEAD_DIM), q_norm_g), rows, cols)
    qd = qd.reshape(b, D_KV_HEADS, D_GROUP, s, HEAD_DIM)
    kd = axial_rope(rms_norm(to_heads(kd, D_KV_HEADS, HEAD_DIM), k_norm_g), rows, cols)
    vd = to_heads(vd, D_KV_HEADS, HEAD_DIM)

    def d_block(qblk):
        sc = jnp.einsum('bkgqd,bksd->bkgqs', qblk, kd, preferred_element_type=f32) * scale
        return jnp.einsum('bkgqs,bksd->bkgqd', jax.nn.softmax(sc, -1).astype(vd.dtype), vd)

    od = sweep_query_blocks(d_block, (qd,))
    od = from_heads(od.reshape(b, D_HEADS, s, HEAD_DIM))
    return jnp.concatenate([oc, od], -1) @ w_out


def peer(x, w_q, sub_keys, u, v):
    b, s, d = x.shape
    t = b * s
    xt = x.reshape(t, d)
    q = (xt @ w_q).reshape(t, PEER_HEADS, 2, PEER_HALF)
    sc = jnp.einsum('thcd,hcnd->thcn', q, sub_keys, preferred_element_type=jnp.float32)
    sv, si = lax.top_k(sc, PEER_TOPK)
    cand = (sv[:, :, 0, :, None] + sv[:, :, 1, None, :]).reshape(t, PEER_HEADS, PEER_TOPK * PEER_TOPK)
    cidx = (si[:, :, 0, :, None] * PEER_NKEYS + si[:, :, 1, None, :]).reshape(t, PEER_HEADS, PEER_TOPK * PEER_TOPK)
    top_s, top_i = lax.top_k(cand, PEER_TOPK)
    eidx = jnp.take_along_axis(cidx, top_i, axis=-1)
    gate = jax.nn.softmax(top_s, axis=-1)
    n_chunks = t // PEER_CHUNK
    xc = xt.reshape(n_chunks, PEER_CHUNK, d)
    ec = eidx.reshape(n_chunks, PEER_CHUNK, PEER_HEADS * PEER_TOPK)
    gc = gate.reshape(n_chunks, PEER_CHUNK, PEER_HEADS * PEER_TOPK)

    def chunk(args):
        xk, ek, gk = args
        hid = jax.nn.gelu(jnp.einsum('cd,ced->ce', xk, u[ek], preferred_element_type=jnp.float32))
        return jnp.einsum('ce,ced->cd', (gk * hid).astype(v.dtype), v[ek])

    return lax.map(chunk, (xc, ec, gc)).reshape(b, s, d)


def setup_inputs(seed: int = 0) -> dict:
    key = jax.random.key(seed)
    ks = jax.random.split(key, 24)
    nrm = lambda k, shape, scale: jax.random.normal(k, shape, jnp.float32) * scale
    gain = lambda k, shape: 1.0 + nrm(k, shape, 0.02)
    return {
        "x": nrm(ks[0], (BATCH, SEQ, D_MODEL), 1.0),
        "w_in_ab": nrm(ks[1], (N_EVEN, D_MODEL, sum(AB_SIZES)), D_MODEL ** -0.5),
        "sink_a": nrm(ks[2], (N_EVEN, A_KV_HEADS, A_GROUP), 0.5),
        "w_out_ab": nrm(ks[3], (N_EVEN, AB_OUT, D_MODEL), DEEPNORM_BETA * AB_OUT ** -0.5),
        "w_in_cd": nrm(ks[4], (N_ODD, D_MODEL, sum(CD_SIZES)), D_MODEL ** -0.5),
        "lam_q1": nrm(ks[5], (N_ODD, HEAD_DIM), 0.1),
        "lam_k1": nrm(ks[6], (N_ODD, HEAD_DIM), 0.1),
        "lam_q2": nrm(ks[7], (N_ODD, HEAD_DIM), 0.1),
        "lam_k2": nrm(ks[8], (N_ODD, HEAD_DIM), 0.1),
        "subln_g": gain(ks[9], (N_ODD, 2 * HEAD_DIM)),
        "q_norm_g": gain(ks[10], (N_ODD, HEAD_DIM)),
        "k_norm_g": gain(ks[11], (N_ODD, HEAD_DIM)),
        "w_out_cd": nrm(ks[12], (N_ODD, CD_OUT, D_MODEL), DEEPNORM_BETA * CD_OUT ** -0.5),
        "ln_mix_g": gain(ks[13], (DEPTH, D_MODEL)),
        "ln_mix_b": nrm(ks[14], (DEPTH, D_MODEL), 0.02),
        "peer_wq": nrm(ks[15], (DEPTH, D_MODEL, PEER_HEADS * PEER_QDIM), D_MODEL ** -0.5),
        "peer_keys": nrm(ks[16], (DEPTH, PEER_HEADS, 2, PEER_NKEYS, PEER_HALF), PEER_HALF ** -0.5),
        "peer_u": nrm(ks[17], (DEPTH, PEER_EXPERTS, D_MODEL), D_MODEL ** -0.5),
        "peer_v": nrm(ks[18], (DEPTH, PEER_EXPERTS, D_MODEL), DEEPNORM_BETA * PEER_HEADS ** -0.5),
        "ln_ffn_g": gain(ks[19], (DEPTH, D_MODEL)),
        "ln_ffn_b": nrm(ks[20], (DEPTH, D_MODEL), 0.02),
    }


def reference(x, w_in_ab, sink_a, w_out_ab, w_in_cd, lam_q1, lam_k1, lam_q2, lam_k2, subln_g,
              q_norm_g, k_norm_g, w_out_cd, ln_mix_g, ln_mix_b, peer_wq, peer_keys, peer_u, peer_v,
              ln_ffn_g, ln_ffn_b):
    b, s, _ = x.shape
    pos = jnp.arange(s)
    ROWS = s // GRID_W
    rows = jnp.repeat(jnp.arange(ROWS), GRID_W)
    cols = jnp.tile(jnp.arange(GRID_W), ROWS)
    for layer in range(DEPTH):
        i = layer // 2
        if layer % 2 == 0:
            mix = even_mixer(x, w_in_ab[i], sink_a[i], w_out_ab[i], pos)
        else:
            lambda_init = 0.8 - 0.6 * math.exp(-0.3 * layer)
            mix = odd_mixer(x, w_in_cd[i], lam_q1[i], lam_k1[i], lam_q2[i], lam_k2[i], subln_g[i],
                            q_norm_g[i], k_norm_g[i], w_out_cd[i], pos, rows, cols, lambda_init)
        x = layer_norm(DEEPNORM_ALPHA * x + mix, ln_mix_g[layer], ln_mix_b[layer])
        ffn = peer(x, peer_wq[layer], peer_keys[layer], peer_u[layer], peer_v[layer])
        x = layer_norm(DEEPNORM_ALPHA * x + ffn, ln_ffn_g[layer], ln_ffn_b[layer])
    return x
```

```python
import functools
import math

import jax
import jax.numpy as jnp
from jax import lax
from jax.experimental import pallas as pl
from jax.experimental.pallas import tpu as pltpu

D_MODEL = 4096
DEPTH = 2
HEAD_DIM = 128
N_HEADS = D_MODEL // HEAD_DIM
ROPE_DIMS = HEAD_DIM // 4
ROPE_THETA = 500000.0
AXIAL_THETA = 10000.0
GRID_W = 64
A_HEADS = N_HEADS // 2
A_KV_HEADS = A_HEADS // 4
A_GROUP = A_HEADS // A_KV_HEADS
A_WINDOW = 128
B_PAIRS = ((128, 1), (512, 4), (2048, 16))
B_GROUP_HEADS = N_HEADS // 4
B_HEADS = len(B_PAIRS) * B_GROUP_HEADS
C_HEADS = N_HEADS // 4
D_HEADS = N_HEADS // 2
D_KV_HEADS = D_HEADS // 4
D_GROUP = D_HEADS // D_KV_HEADS
PEER_HEADS = 8
PEER_NKEYS = 128
PEER_EXPERTS = PEER_NKEYS * PEER_NKEYS
PEER_QDIM = 256
PEER_TOPK = 16
DEEPNORM_ALPHA = (2 * DEPTH) ** 0.25
LN_EPS = 1e-5
RMS_EPS = 1e-6
NEG_INF = -1e30
LOG2E = 1.4426950408889634
QK_SCALE = HEAD_DIM ** -0.5

AB_SIZES = (A_HEADS * HEAD_DIM, A_KV_HEADS * HEAD_DIM, A_KV_HEADS * HEAD_DIM,
            B_HEADS * HEAD_DIM, B_HEADS * HEAD_DIM, B_HEADS * HEAD_DIM)
CD_SIZES = (C_HEADS * 2 * HEAD_DIM, C_HEADS * 2 * HEAD_DIM, C_HEADS * 2 * HEAD_DIM,
            D_HEADS * HEAD_DIM, D_KV_HEADS * HEAD_DIM, D_KV_HEADS * HEAD_DIM)

LANES = 128
V7X_VMEM_BYTES = 64 * 1024 * 1024


def _vmem_limit(nbytes):
    return int(min(max(nbytes * 5 // 4 + (4 << 20), 16 << 20), V7X_VMEM_BYTES - (6 << 20)))


def _cparams(sem, nbytes):
    return pltpu.CompilerParams(dimension_semantics=sem, vmem_limit_bytes=_vmem_limit(nbytes))


def _dot(a, b):
    return jnp.dot(a, b, preferred_element_type=jnp.float32)


def _dot_nt(a, b):
    return lax.dot_general(a, b, (((1,), (1,)), ((), ())), preferred_element_type=jnp.float32)


def _rope_tables(spans):
    s = spans[0][2].shape[0]
    c = jnp.ones((s, HEAD_DIM), jnp.float32)
    sa = jnp.zeros((s, HEAD_DIM), jnp.float32)
    sb = jnp.zeros((s, HEAD_DIM), jnp.float32)
    for start, n, pos, theta in spans:
        half = n // 2
        inv = theta ** (-jnp.arange(half, dtype=jnp.float32) / half)
        ang = pos.astype(jnp.float32)[:, None] * inv[None, :]
        cos, sin = jnp.cos(ang), jnp.sin(ang)
        c = c.at[:, start:start + half].set(cos).at[:, start + half:start + n].set(cos)
        sa = sa.at[:, start:start + half].set(-sin)
        sb = sb.at[:, start + half:start + n].set(sin)
    return jnp.stack([c, sa, sb])


def _proj_kernel(col_modes, tn, a_ref, b_ref, tabs_ref, gains_ref, o_ref):
    j = pl.program_id(1)
    acc = _dot(a_ref[...], b_ref[...])

    def epilogue(mode):
        tab, half, scale, gain = mode
        for c in range(tn // HEAD_DIM):
            sl = slice(c * HEAD_DIM, (c + 1) * HEAD_DIM)
            xc = acc[:, sl]
            if gain is not None:
                ms = jnp.mean(xc * xc, axis=-1, keepdims=True)
                xc = xc * lax.rsqrt(ms + RMS_EPS) * gains_ref[gain:gain + 1, :]
            if tab is not None:
                xc = (xc * tabs_ref[3 * tab] + pltpu.roll(xc, HEAD_DIM - half, 1) * tabs_ref[3 * tab + 1]
                      + pltpu.roll(xc, half, 1) * tabs_ref[3 * tab + 2])
            if scale != 1.0:
                xc = xc * scale
            o_ref[:, sl] = xc.astype(o_ref.dtype)

    for lo, hi, mode in col_modes:
        pl.when((j >= lo) & (j < hi))(functools.partial(epilogue, mode))


def _project(a, b, tabs, gains, col_modes, *, tm, tn):
    s, k = a.shape
    n = b.shape[1]
    nt = tabs.shape[0]
    est = 2 * (tm * k * 2 + k * tn * 2 + tm * tn * 2 + nt * tm * HEAD_DIM * 4) + 3 * tm * tn * 4
    return pl.pallas_call(
        functools.partial(_proj_kernel, col_modes, tn),
        out_shape=jax.ShapeDtypeStruct((s, n), jnp.bfloat16),
        grid=(s // tm, n // tn),
        in_specs=[pl.BlockSpec((tm, k), lambda i, j: (i, 0)),
                  pl.BlockSpec((k, tn), lambda i, j: (0, j)),
                  pl.BlockSpec((nt, tm, HEAD_DIM), lambda i, j: (0, i, 0)),
                  pl.BlockSpec(gains.shape, lambda i, j: (0, 0))],
        out_specs=pl.BlockSpec((tm, tn), lambda i, j: (i, j)),
        compiler_params=_cparams(("parallel", "arbitrary"), est),
        name="in_proj",
    )(a, b, tabs, gains)


def _band_mask(n, blk, window, seq_len):
    rows = lax.broadcasted_iota(jnp.int32, (blk, 3 * blk), 0)
    cols = lax.broadcasted_iota(jnp.int32, (blk, 3 * blk), 1)
    kpos = (n - 1) * blk + cols
    dist = jnp.abs(rows + blk - cols)
    return (dist <= window) & (kpos >= 0) & (kpos < seq_len)


def _window_kernel(blk, window, seq_len, group, sink_ref, q_ref, kp_ref, kc_ref, kn_ref,
                   vp_ref, vc_ref, vn_ref, o_ref):
    kh = pl.program_id(0)
    n = pl.program_id(1)
    mask = _band_mask(n, blk, window, seq_len)
    kcat = jnp.concatenate([kp_ref[...], kc_ref[...], kn_ref[...]], axis=0)
    vcat = jnp.concatenate([vp_ref[...], vc_ref[...], vn_ref[...]], axis=0)
    for g in range(group):
        sl = slice(g * HEAD_DIM, (g + 1) * HEAD_DIM)
        s = jnp.where(mask, _dot_nt(q_ref[:, sl], kcat), NEG_INF)
        sink = sink_ref[kh * group + g] * LOG2E
        m = jnp.maximum(jnp.max(s, axis=-1, keepdims=True), sink)
        p = jnp.exp2(s - m)
        denom = jnp.sum(p, axis=-1, keepdims=True) + jnp.exp2(sink - m)
        o = _dot(p.astype(jnp.bfloat16), vcat) / denom
        o_ref[:, sl] = o.astype(o_ref.dtype)


def _window_attention(qkv, sink, *, q_col, k_col, v_col, n_kv, group, blk, window):
    s = qkv.shape[0]
    nb = s // blk
    prev = lambda n: jnp.maximum(n - 1, 0)
    nxt = lambda n: jnp.minimum(n + 1, nb - 1)
    kv_spec = lambda col, f: pl.BlockSpec((blk, HEAD_DIM), lambda kh, n: (f(n), col + kh))
    ident = lambda n: n
    est = 2 * (2 * blk * group * HEAD_DIM * 2 + 6 * blk * HEAD_DIM * 2) + 8 * blk * 3 * blk * 4
    return pl.pallas_call(
        functools.partial(_window_kernel, blk, window, s, group),
        out_shape=jax.ShapeDtypeStruct((s, n_kv * group * HEAD_DIM), jnp.bfloat16),
        grid=(n_kv, nb),
        in_specs=[pl.BlockSpec(memory_space=pltpu.SMEM),
                  pl.BlockSpec((blk, group * HEAD_DIM), lambda kh, n: (n, q_col // group + kh)),
                  kv_spec(k_col, prev), kv_spec(k_col, ident), kv_spec(k_col, nxt),
                  kv_spec(v_col, prev), kv_spec(v_col, ident), kv_spec(v_col, nxt)],
        out_specs=pl.BlockSpec((blk, group * HEAD_DIM), lambda kh, n: (n, kh)),
        compiler_params=_cparams(("parallel", "arbitrary"), est),
        name="window_attn",
    )(sink, qkv, qkv, qkv, qkv, qkv, qkv, qkv)


def _dilated_kernel(blk, seq_len, heads, q_ref, kp_ref, kc_ref, kn_ref, vp_ref, vc_ref, vn_ref,
                    o_ref, lse_ref):
    n = pl.program_id(0)
    mask = _band_mask(n, blk, blk, seq_len)
    for h in range(heads):
        sl = slice(h * HEAD_DIM, (h + 1) * HEAD_DIM)
        kcat = jnp.concatenate([kp_ref[:, sl], kc_ref[:, sl], kn_ref[:, sl]], axis=0)
        vcat = jnp.concatenate([vp_ref[:, sl], vc_ref[:, sl], vn_ref[:, sl]], axis=0)
        s = jnp.where(mask, _dot_nt(q_ref[:, sl], kcat), NEG_INF)
        m = jnp.max(s, axis=-1, keepdims=True)
        p = jnp.exp2(s - m)
        denom = jnp.sum(p, axis=-1, keepdims=True)
        o_ref[:, sl] = _dot(p.astype(jnp.bfloat16), vcat) / denom
        lse_ref[:, sl] = jnp.broadcast_to(m + jnp.log2(denom), (blk, HEAD_DIM))


def _dilated_attention(q, k, v, *, dil, blk):
    s, width = q.shape
    heads = width // HEAD_DIM
    fold_len = s // dil
    nb = fold_len // blk
    fold = lambda t: t.reshape(fold_len, dil * width)
    spec = lambda f: pl.BlockSpec((blk, width), lambda n, r: (f(n), r))
    prev = lambda n: jnp.maximum(n - 1, 0)
    nxt = lambda n: jnp.minimum(n + 1, nb - 1)
    ident = lambda n: n
    est = 2 * (7 * blk * width * 2 + 2 * blk * width * 4) + 8 * blk * 3 * blk * 4
    qf, kf, vf = fold(q), fold(k), fold(v)
    o, lse = pl.pallas_call(
        functools.partial(_dilated_kernel, blk, fold_len, heads),
        out_shape=[jax.ShapeDtypeStruct((fold_len, dil * width), jnp.float32)] * 2,
        grid=(nb, dil),
        in_specs=[spec(ident), spec(prev), spec(ident), spec(nxt), spec(prev), spec(ident), spec(nxt)],
        out_specs=[spec(ident), spec(ident)],
        compiler_params=_cparams(("parallel", "arbitrary"), est),
        name=f"dilated_attn_{dil}",
    )(qf, kf, kf, kf, vf, vf, vf)
    return o.reshape(s, width), lse.reshape(s, width)


def _merge_kernel(n_groups, *refs):
    o_refs, l_refs, out_ref = refs[:n_groups], refs[n_groups:2 * n_groups], refs[2 * n_groups]
    lses = [r[...] for r in l_refs]
    m = functools.reduce(jnp.maximum, lses)
    ws = [jnp.exp2(l - m) for l in lses]
    tot = functools.reduce(jnp.add, ws)
    acc = functools.reduce(jnp.add, [w * r[...] for w, r in zip(ws, o_refs)])
    out_ref[...] = (acc / tot).astype(out_ref.dtype)


def _merge_groups(outs, lses, *, tm):
    s, width = outs[0].shape
    n = len(outs)
    spec = pl.BlockSpec((tm, width), lambda i: (i, 0))
    return pl.pallas_call(
        functools.partial(_merge_kernel, n),
        out_shape=jax.ShapeDtypeStruct((s, width), jnp.bfloat16),
        grid=(s // tm,),
        in_specs=[spec] * (2 * n),
        out_specs=spec,
        compiler_params=_cparams(("parallel",), 2 * (2 * n * tm * width * 4 + tm * width * 2)),
        name="dilated_merge",
    )(*outs, *lses)


def _online_step(q, k, v, m_ref, l_ref, acc_ref):
    s = _dot_nt(q, k)
    m_prev = m_ref[...]
    m_new = jnp.maximum(m_prev, jnp.max(s, axis=-1, keepdims=True))
    alpha = jnp.exp2(m_prev - m_new)
    p = jnp.exp2(s - m_new)
    l_ref[...] = alpha * l_ref[...] + jnp.sum(p, axis=-1, keepdims=True)
    acc_ref[...] = alpha * acc_ref[...] + _dot(p.astype(jnp.bfloat16), v)
    m_ref[...] = m_new


def _diff_kernel(out_scale, lam_ref, q1_ref, q2_ref, k1_ref, k2_ref, v_ref, g_ref, o_ref,
                 m_ref, l_ref, acc_ref):
    kb = pl.program_id(2)

    @pl.when(kb == 0)
    def _():
        m_ref[...] = jnp.full(m_ref.shape, NEG_INF, jnp.float32)
        l_ref[...] = jnp.zeros(l_ref.shape, jnp.float32)
        acc_ref[...] = jnp.zeros(acc_ref.shape, jnp.float32)

    v = v_ref[...]
    _online_step(q1_ref[...], k1_ref[...], v, m_ref.at[0], l_ref.at[0], acc_ref.at[0])
    _online_step(q2_ref[...], k2_ref[...], v, m_ref.at[1], l_ref.at[1], acc_ref.at[1])

    @pl.when(kb == pl.num_programs(2) - 1)
    def _():
        o = acc_ref[0] / l_ref[0] - lam_ref[0] * (acc_ref[1] / l_ref[1])
        ms = jnp.mean(o * o, axis=-1, keepdims=True)
        o_ref[...] = (o * lax.rsqrt(ms + RMS_EPS) * g_ref[...] * out_scale).astype(o_ref.dtype)


def _diff_attention(qkv, lam, subln_g, out_scale, *, q_col, k_col, v_col, heads, tq, tk):
    s = qkv.shape[0]
    dv = 2 * HEAD_DIM
    q_spec = lambda off: pl.BlockSpec((tq, HEAD_DIM), lambda h, i, j: (i, q_col + 2 * h + off))
    k_spec = lambda off: pl.BlockSpec((tk, HEAD_DIM), lambda h, i, j: (j, k_col + 2 * h + off))
    est = (2 * (2 * tq * HEAD_DIM * 2 + 2 * tk * HEAD_DIM * 2 + tk * dv * 2 + tq * dv * 2)
           + 2 * tq * (dv + 2 * LANES) * 4 + 6 * tq * tk * 4)
    return pl.pallas_call(
        functools.partial(_diff_kernel, out_scale),
        out_shape=jax.ShapeDtypeStruct((s, heads * dv), jnp.bfloat16),
        grid=(heads, s // tq, s // tk),
        in_specs=[pl.BlockSpec(memory_space=pltpu.SMEM),
                  q_spec(0), q_spec(1), k_spec(0), k_spec(1),
                  pl.BlockSpec((tk, dv), lambda h, i, j: (j, v_col // 2 + h)),
                  pl.BlockSpec((1, dv), lambda h, i, j: (0, 0))],
        out_specs=pl.BlockSpec((tq, dv), lambda h, i, j: (i, h)),
        scratch_shapes=[pltpu.VMEM((2, tq, 1), jnp.float32), pltpu.VMEM((2, tq, 1), jnp.float32),
                        pltpu.VMEM((2, tq, dv), jnp.float32)],
        compiler_params=_cparams(("parallel", "parallel", "arbitrary"), est),
        name="diff_attn",
    )(lam, qkv, qkv, qkv, qkv, qkv, subln_g)


def _gqa_kernel(group, q_ref, k_ref, v_ref, o_ref, m_ref, l_ref, acc_ref):
    kb = pl.program_id(2)

    @pl.when(kb == 0)
    def _():
        m_ref[...] = jnp.full(m_ref.shape, NEG_INF, jnp.float32)
        l_ref[...] = jnp.zeros(l_ref.shape, jnp.float32)
        acc_ref[...] = jnp.zeros(acc_ref.shape, jnp.float32)

    k, v = k_ref[...], v_ref[...]
    for g in range(group):
        _online_step(q_ref[:, g * HEAD_DIM:(g + 1) * HEAD_DIM], k, v, m_ref.at[g], l_ref.at[g], acc_ref.at[g])

    @pl.when(kb == pl.num_programs(2) - 1)
    def _():
        for g in range(group):
            o_ref[:, g * HEAD_DIM:(g + 1) * HEAD_DIM] = (acc_ref[g] / l_ref[g]).astype(o_ref.dtype)


def _gqa_attention(qkv, *, q_col, k_col, v_col, n_kv, group, tq, tk):
    s = qkv.shape[0]
    est = (2 * (2 * tq * group * HEAD_DIM * 2 + 2 * tk * HEAD_DIM * 2)
           + group * tq * (HEAD_DIM + 2 * LANES) * 4 + 6 * tq * tk * 4)
    return pl.pallas_call(
        functools.partial(_gqa_kernel, group),
        out_shape=jax.ShapeDtypeStruct((s, n_kv * group * HEAD_DIM), jnp.bfloat16),
        grid=(n_kv, s // tq, s // tk),
        in_specs=[pl.BlockSpec((tq, group * HEAD_DIM), lambda kh, i, j: (i, q_col // group + kh)),
                  pl.BlockSpec((tk, HEAD_DIM), lambda kh, i, j: (j, k_col + kh)),
                  pl.BlockSpec((tk, HEAD_DIM), lambda kh, i, j: (j, v_col + kh))],
        out_specs=pl.BlockSpec((tq, group * HEAD_DIM), lambda kh, i, j: (i, kh)),
        scratch_shapes=[pltpu.VMEM((group, tq, 1), jnp.float32), pltpu.VMEM((group, tq, 1), jnp.float32),
                        pltpu.VMEM((group, tq, HEAD_DIM), jnp.float32)],
        compiler_params=_cparams(("parallel", "parallel", "arbitrary"), est),
        name="gqa_attn",
    )(qkv, qkv, qkv)


def _ln_inplace(of_ref, ob_ref, g_ref, b_ref, tn):
    d = of_ref.shape[1]
    chunks = [slice(c * tn, (c + 1) * tn) for c in range(d // tn)]
    tot = functools.reduce(jnp.add, [jnp.sum(of_ref[:, sl], axis=-1, keepdims=True) for sl in chunks])
    mu = tot * (1.0 / d)
    sq = functools.reduce(jnp.add, [jnp.sum(jnp.square(of_ref[:, sl] - mu), axis=-1, keepdims=True)
                                    for sl in chunks])
    rstd = lax.rsqrt(sq * (1.0 / d) + LN_EPS)
    for sl in chunks:
        out = (of_ref[:, sl] - mu) * rstd * g_ref[:, sl] + b_ref[:, sl]
        of_ref[:, sl] = out
        ob_ref[:, sl] = out.astype(jnp.bfloat16)


def _outproj_kernel(tn, a_ref, w_ref, x_ref, g_ref, b_ref, of_ref, ob_ref):
    j = pl.program_id(1)
    d = of_ref.shape[1]
    nj = d // tn
    y = _dot(a_ref[...], w_ref[...]) + DEEPNORM_ALPHA * x_ref[...]
    for c in range(nj):
        @pl.when(j == c)
        def _(c=c):
            of_ref[:, c * tn:(c + 1) * tn] = y

    @pl.when(j == nj - 1)
    def _():
        _ln_inplace(of_ref, ob_ref, g_ref, b_ref, tn)


def _outproj_ln(a, w, x, g, b, *, tm, tn):
    s, k = a.shape
    d = w.shape[1]
    est = 2 * (tm * k * 2 + k * tn * 2 + tm * tn * 4 + tm * d * 4 + tm * d * 2) + 4 * tm * tn * 4
    row = pl.BlockSpec((tm, d), lambda i, j: (i, 0))
    vec = pl.BlockSpec((1, d), lambda i, j: (0, 0))
    return pl.pallas_call(
        functools.partial(_outproj_kernel, tn),
        out_shape=[jax.ShapeDtypeStruct((s, d), jnp.float32), jax.ShapeDtypeStruct((s, d), jnp.bfloat16)],
        grid=(s // tm, d // tn),
        in_specs=[pl.BlockSpec((tm, k), lambda i, j: (i, 0)),
                  pl.BlockSpec((k, tn), lambda i, j: (0, j)),
                  pl.BlockSpec((tm, tn), lambda i, j: (i, j)), vec, vec],
        out_specs=[row, row],
        compiler_params=_cparams(("parallel", "arbitrary"), est),
        name="out_proj_ln",
    )(a, w, x, g, b)


def _top_rows(a, k):
    vals = []
    for _ in range(k):
        m = jnp.max(a, axis=0, keepdims=True)
        vals.append(m)
        a = jnp.where(a == m, -jnp.inf, a)
    return vals


def _router_kernel(x_ref, wq_ref, keys_ref, tau_ref, e1_ref, s2_ref, e2_ref):
    half = PEER_QDIM // 2
    q = _dot(x_ref[...], wq_ref[...]).astype(jnp.bfloat16)
    s1 = _dot_nt(keys_ref[0, 0], q[:, :half])
    s2 = _dot_nt(keys_ref[0, 1], q[:, half:])
    k = PEER_TOPK
    v1 = _top_rows(s1, k + 1)
    v2 = _top_rows(s2, k + 1)
    rid = lax.broadcasted_iota(jnp.int32, (k, s1.shape[1]), 0)
    sv2 = functools.reduce(lambda acc, i: jnp.where(rid == i, v2[i], acc), range(1, k),
                           jnp.broadcast_to(v2[0], rid.shape))
    cand = jnp.concatenate([v1[a] + sv2 for a in range(k)], axis=0)
    c = _top_rows(cand, k + 1)
    c_next = jnp.maximum(c[k], jnp.maximum(v1[k] + v2[0], v1[0] + v2[k]))
    thr = 0.5 * (c[k - 1] + c_next)
    ee2 = jnp.exp(sv2 - v2[0])
    z = functools.reduce(jnp.add, [
        jnp.exp(v1[a] - v1[0]) * jnp.sum(jnp.where(sv2 >= thr - v1[a], ee2, 0.0), axis=0, keepdims=True)
        for a in range(k)])
    tau_ref[0] = thr - s1
    e1_ref[0] = jnp.exp(s1 - v1[0]) / z
    s2_ref[0] = s2
    e2_ref[0] = jnp.exp(s2 - v2[0])


def _peer_router(xb, wq, keys, *, tt):
    t, d = xb.shape
    out = jax.ShapeDtypeStruct((PEER_HEADS, PEER_NKEYS, t), jnp.float32)
    ospec = pl.BlockSpec((1, PEER_NKEYS, tt), lambda i, h: (h, 0, i))
    est = 2 * (tt * d * 2 + d * PEER_QDIM * 2 + 4 * PEER_NKEYS * tt * 4) + 24 * PEER_QDIM * tt * 4
    return pl.pallas_call(
        _router_kernel,
        out_shape=[out] * 4,
        grid=(t // tt, PEER_HEADS),
        in_specs=[pl.BlockSpec((tt, d), lambda i, h: (i, 0)),
                  pl.BlockSpec((d, PEER_QDIM), lambda i, h: (0, h)),
                  pl.BlockSpec((1, 2, PEER_NKEYS, PEER_QDIM // 2), lambda i, h: (h, 0, 0, 0))],
        out_specs=[ospec] * 4,
        compiler_params=_cparams(("parallel", "arbitrary"), est),
        name="peer_router",
    )(xb, wq, keys)


def _gelu_tanh(h):
    c = math.sqrt(2.0 / math.pi)
    return 0.5 * h * (1.0 + jnp.tanh(c * (h + 0.044715 * (h * h * h))))


def _peer_dense_kernel(x_ref, u_ref, vt_ref, tau_ref, e1_ref, s2_ref, e2_ref, o_ref, p_ref):
    et = pl.program_id(1)
    n_i = u_ref.shape[0] // PEER_NKEYS

    @pl.when(et == 0)
    def _():
        o_ref[...] = jnp.zeros(o_ref.shape, jnp.float32)

    hid = _dot_nt(u_ref[...], x_ref[...])
    for il in range(n_i):
        i = et * n_i + il
        rows = slice(il * PEER_NKEYS, (il + 1) * PEER_NKEYS)
        gate = None
        for h in range(PEER_HEADS):
            tau = tau_ref[h, pl.ds(i, 1), :]
            e1 = e1_ref[h, pl.ds(i, 1), :]
            w = jnp.where(s2_ref[h] >= tau, e2_ref[h], 0.0) * e1
            gate = w if gate is None else gate + w
        p_ref[rows, :] = (gate * _gelu_tanh(hid[rows, :])).astype(jnp.bfloat16)
    o_ref[...] += _dot(vt_ref[...], p_ref[...])


def _peer_dense(xb, ub, vtb, tau, e1, s2, e2, *, tt, te):
    t, d = xb.shape
    n_exp = ub.shape[0]
    once = pl.Buffered(1)
    rspec = pl.BlockSpec((PEER_HEADS, PEER_NKEYS, tt), lambda i, e: (0, 0, i), pipeline_mode=once)
    est = (tt * d * 2 + 2 * te * d * 2 + 2 * d * te * 2 + 4 * PEER_HEADS * PEER_NKEYS * tt * 4
           + 2 * d * tt * 4 + te * tt * 2 + 4 * te * tt * 4)
    return pl.pallas_call(
        _peer_dense_kernel,
        out_shape=jax.ShapeDtypeStruct((d, t), jnp.float32),
        grid=(t // tt, n_exp // te),
        in_specs=[pl.BlockSpec((tt, d), lambda i, e: (i, 0), pipeline_mode=once),
                  pl.BlockSpec((te, d), lambda i, e: (e, 0)),
                  pl.BlockSpec((d, te), lambda i, e: (0, e)),
                  rspec, rspec, rspec, rspec],
        out_specs=pl.BlockSpec((d, tt), lambda i, e: (0, i)),
        scratch_shapes=[pltpu.VMEM((te, tt), jnp.bfloat16)],
        compiler_params=_cparams(("parallel", "arbitrary"), est),
        name="peer_dense",
    )(xb, ub, vtb, tau, e1, s2, e2)


def _ffn_ln_kernel(tn, x_ref, ft_ref, g_ref, b_ref, of_ref, ob_ref):
    for c in range(of_ref.shape[1] // tn):
        sl = slice(c * tn, (c + 1) * tn)
        of_ref[:, sl] = DEEPNORM_ALPHA * x_ref[:, sl] + ft_ref[sl, :].T
    _ln_inplace(of_ref, ob_ref, g_ref, b_ref, tn)


def _ffn_ln(x, ffn_t, g, b, *, tm):
    t, d = x.shape
    row = pl.BlockSpec((tm, d), lambda i: (i, 0))
    vec = pl.BlockSpec((1, d), lambda i: (0, 0))
    return pl.pallas_call(
        functools.partial(_ffn_ln_kernel, 512),
        out_shape=[jax.ShapeDtypeStruct((t, d), jnp.float32), jax.ShapeDtypeStruct((t, d), jnp.bfloat16)],
        grid=(t // tm,),
        in_specs=[row, pl.BlockSpec((d, tm), lambda i: (0, i)), vec, vec],
        out_specs=[row, row],
        compiler_params=_cparams(("parallel",), 2 * (3 * tm * d * 4 + tm * d * 2) + 3 * tm * d * 4),
        name="ffn_ln",
    )(x, ffn_t, g, b)


def _col_tiles(sizes, modes, tn):
    out, lo = [], 0
    for size, mode in zip(sizes, modes):
        assert size % tn == 0
        out.append((lo // tn, (lo + size) // tn, mode))
        lo += size
    return out


def _even_mixer(xb, w_in, sink, tabs):
    q_scale = QK_SCALE * LOG2E
    rope_q, rope_k, plain = (0, ROPE_DIMS // 2, q_scale, None), (0, ROPE_DIMS // 2, 1.0, None), (None, 0, 1.0, None)
    modes = _col_tiles(AB_SIZES, [rope_q, rope_k, plain, rope_q, rope_k, plain], 512)
    gains = jnp.ones((8, HEAD_DIM), jnp.float32)
    qkv = _project(xb, w_in, tabs, gains, modes, tm=1024 if xb.shape[0] % 1024 == 0 else xb.shape[0], tn=512)
    cols = [0]
    for sz in AB_SIZES:
        cols.append(cols[-1] + sz // HEAD_DIM)
    oa = _window_attention(qkv, sink.reshape(-1), q_col=cols[0], k_col=cols[1], v_col=cols[2],
                           n_kv=A_KV_HEADS, group=A_GROUP, blk=A_WINDOW, window=A_WINDOW)
    gw = B_GROUP_HEADS * HEAD_DIM
    outs, lses = [], []
    for g, (win, dil) in enumerate(B_PAIRS):
        take = lambda c: qkv[:, (c * HEAD_DIM + g * gw):(c * HEAD_DIM + (g + 1) * gw)]
        o, l = _dilated_attention(take(cols[3]), take(cols[4]), take(cols[5]), dil=dil, blk=win // (2 * dil))
        outs.append(o)
        lses.append(l)
    ob = _merge_groups(outs, lses, tm=512)
    return jnp.concatenate([oa, ob], axis=-1)


def _odd_mixer(xb, w_in, lam, subln_g, q_norm_g, k_norm_g, tabs, lambda_init):
    q_scale = QK_SCALE * LOG2E
    half_ax = HEAD_DIM // 4
    rope_q, rope_k, plain = (0, ROPE_DIMS // 2, q_scale, None), (0, ROPE_DIMS // 2, 1.0, None), (None, 0, 1.0, None)
    ax_q, ax_k = (1, half_ax, q_scale, 0), (1, half_ax, 1.0, 1)
    modes = _col_tiles(CD_SIZES, [rope_q, rope_k, plain, ax_q, ax_k, plain], 512)
    gains = jnp.zeros((8, HEAD_DIM), jnp.float32).at[0].set(q_norm_g).at[1].set(k_norm_g)
    qkv = _project(xb, w_in, tabs, gains, modes, tm=1024 if xb.shape[0] % 1024 == 0 else xb.shape[0], tn=512)
    cols = [0]
    for sz in CD_SIZES:
        cols.append(cols[-1] + sz // HEAD_DIM)
    s = xb.shape[0]
    oc = _diff_attention(qkv, lam, subln_g.reshape(1, -1), 1.0 - lambda_init, q_col=cols[0], k_col=cols[1],
                         v_col=cols[2], heads=C_HEADS, tq=min(512, s), tk=min(1024, s))
    od = _gqa_attention(qkv, q_col=cols[3], k_col=cols[4], v_col=cols[5], n_kv=D_KV_HEADS, group=D_GROUP,
                        tq=min(256, s), tk=min(1024, s))
    return jnp.concatenate([oc, od], axis=-1)


def _peer_ffn(xf, xb, wq, keys, u, v, g, b):
    t = xb.shape[0]
    tt = min(512, t)
    tau, e1, s2, e2 = _peer_router(xb, wq.astype(jnp.bfloat16), keys.astype(jnp.bfloat16), tt=tt)
    ffn_t = _peer_dense(xb, u.astype(jnp.bfloat16), v.T.astype(jnp.bfloat16), tau, e1, s2, e2, tt=tt, te=512)
    return _ffn_ln(xf, ffn_t, g.reshape(1, -1), b.reshape(1, -1), tm=min(256, t))


def kernel(x, w_in_ab, sink_a, w_out_ab, w_in_cd, lam_q1, lam_k1, lam_q2, lam_k2, subln_g, q_norm_g, k_norm_g,
           w_out_cd, ln_mix_g, ln_mix_b, peer_wq, peer_keys, peer_u, peer_v, ln_ffn_g, ln_ffn_b):
    b, s, d = x.shape
    assert b == 1 and d == D_MODEL
    bf = jnp.bfloat16
    pos = jnp.arange(s)
    rows, cols = pos // GRID_W, pos % GRID_W
    tab_partial = _rope_tables([(0, ROPE_DIMS, pos, ROPE_THETA)])
    tab_axial = _rope_tables([(0, HEAD_DIM // 2, rows, AXIAL_THETA), (HEAD_DIM // 2, HEAD_DIM // 2, cols, AXIAL_THETA)])
    xf = x[0]
    xb = xf.astype(bf)
    for layer in range(DEPTH):
        i = layer // 2
        if layer % 2 == 0:
            mix_in = _even_mixer(xb, w_in_ab[i].astype(bf), sink_a[i], tab_partial)
            w_out = w_out_ab[i]
        else:
            lambda_init = 0.8 - 0.6 * math.exp(-0.3 * layer)
            f32 = jnp.float32
            lam = (jnp.exp(jnp.sum(lam_q1[i].astype(f32) * lam_k1[i].astype(f32)))
                   - jnp.exp(jnp.sum(lam_q2[i].astype(f32) * lam_k2[i].astype(f32))) + lambda_init)
            mix_in = _odd_mixer(xb, w_in_cd[i].astype(bf), lam.reshape(1), subln_g[i], q_norm_g[i], k_norm_g[i],
                                jnp.concatenate([tab_partial, tab_axial]), lambda_init)
            w_out = w_out_cd[i]
        xf, xb = _outproj_ln(mix_in, w_out.astype(bf), xf, ln_mix_g[layer].reshape(1, -1),
                             ln_mix_b[layer].reshape(1, -1), tm=min(512, s), tn=512)
        xf, xb = _peer_ffn(xf, xb, peer_wq[layer], peer_keys[layer], peer_u[layer], peer_v[layer],
                           ln_ffn_g[layer], ln_ffn_b[layer])
    return xf[None]
```

```python
import functools
import math

import jax
import jax.numpy as jnp
from jax import lax
from jax.experimental import pallas as pl
from jax.experimental.pallas import tpu as pltpu

D_MODEL = 4096
DEPTH = 2
HEAD_DIM = 128
N_HEADS = D_MODEL // HEAD_DIM
ROPE_DIMS = HEAD_DIM // 4
ROPE_THETA = 500000.0
AXIAL_THETA = 10000.0
GRID_W = 64
A_HEADS = N_HEADS // 2
A_KV_HEADS = A_HEADS // 4
A_GROUP = A_HEADS // A_KV_HEADS
A_WINDOW = 128
B_PAIRS = ((128, 1), (512, 4), (2048, 16))
B_GROUP_HEADS = N_HEADS // 4
B_HEADS = len(B_PAIRS) * B_GROUP_HEADS
C_HEADS = N_HEADS // 4
D_HEADS = N_HEADS // 2
D_KV_HEADS = D_HEADS // 4
D_GROUP = D_HEADS // D_KV_HEADS
PEER_HEADS = 8
PEER_NKEYS = 128
PEER_EXPERTS = PEER_NKEYS * PEER_NKEYS
PEER_QDIM = 256
PEER_TOPK = 16
DEEPNORM_ALPHA = (2 * DEPTH) ** 0.25
LN_EPS = 1e-5
RMS_EPS = 1e-6
NEG_INF = -1e30
LOG2E = 1.4426950408889634
QK_SCALE = HEAD_DIM ** -0.5

AB_SIZES = (A_HEADS * HEAD_DIM, A_KV_HEADS * HEAD_DIM, A_KV_HEADS * HEAD_DIM,
            B_HEADS * HEAD_DIM, B_HEADS * HEAD_DIM, B_HEADS * HEAD_DIM)
CD_SIZES = (C_HEADS * 2 * HEAD_DIM, C_HEADS * 2 * HEAD_DIM, C_HEADS * 2 * HEAD_DIM,
            D_HEADS * HEAD_DIM, D_KV_HEADS * HEAD_DIM, D_KV_HEADS * HEAD_DIM)

LANES = 128
V7X_VMEM_BYTES = 64 * 1024 * 1024


def _vmem_limit(nbytes):
    return int(min(max(nbytes * 5 // 4 + (4 << 20), 16 << 20), V7X_VMEM_BYTES - (6 << 20)))


def _cparams(sem, nbytes):
    return pltpu.CompilerParams(dimension_semantics=sem, vmem_limit_bytes=_vmem_limit(nbytes))


def _dot(a, b):
    return jnp.dot(a, b, preferred_element_type=jnp.float32)


def _dot_nt(a, b):
    return lax.dot_general(a, b, (((1,), (1,)), ((), ())), preferred_element_type=jnp.float32)


def _rope_tables(spans):
    s = spans[0][2].shape[0]
    c = jnp.ones((s, HEAD_DIM), jnp.float32)
    sa = jnp.zeros((s, HEAD_DIM), jnp.float32)
    sb = jnp.zeros((s, HEAD_DIM), jnp.float32)
    for start, n, pos, theta in spans:
        half = n // 2
        inv = theta ** (-jnp.arange(half, dtype=jnp.float32) / half)
        ang = pos.astype(jnp.float32)[:, None] * inv[None, :]
        cos, sin = jnp.cos(ang), jnp.sin(ang)
        c = c.at[:, start:start + half].set(cos).at[:, start + half:start + n].set(cos)
        sa = sa.at[:, start:start + half].set(-sin)
        sb = sb.at[:, start + half:start + n].set(sin)
    return jnp.stack([c, sa, sb])


def _proj_kernel(col_modes, tn, a_ref, b_ref, tabs_ref, gains_ref, o_ref):
    j = pl.program_id(1)
    acc = _dot(a_ref[...], b_ref[...])

    def epilogue(mode):
        tab, half, scale, gain = mode
        for c in range(tn // HEAD_DIM):
            sl = slice(c * HEAD_DIM, (c + 1) * HEAD_DIM)
            xc = acc[:, sl]
            if gain is not None:
                ms = jnp.mean(xc * xc, axis=-1, keepdims=True)
                xc = xc * lax.rsqrt(ms + RMS_EPS) * gains_ref[gain:gain + 1, :]
            if tab is not None:
                xc = (xc * tabs_ref[3 * tab] + pltpu.roll(xc, HEAD_DIM - half, 1) * tabs_ref[3 * tab + 1]
                      + pltpu.roll(xc, half, 1) * tabs_ref[3 * tab + 2])
            if scale != 1.0:
                xc = xc * scale
            o_ref[:, sl] = xc.astype(o_ref.dtype)

    for lo, hi, mode in col_modes:
        pl.when((j >= lo) & (j < hi))(functools.partial(epilogue, mode))


def _project(a, b, tabs, gains, col_modes, *, tm, tn):
    s, k = a.shape
    n = b.shape[1]
    nt = tabs.shape[0]
    est = 2 * (tm * k * 2 + k * tn * 2 + tm * tn * 2 + nt * tm * HEAD_DIM * 4) + 3 * tm * tn * 4
    return pl.pallas_call(
        functools.partial(_proj_kernel, col_modes, tn),
        out_shape=jax.ShapeDtypeStruct((s, n), jnp.bfloat16),
        grid=(s // tm, n // tn),
        in_specs=[pl.BlockSpec((tm, k), lambda i, j: (i, 0)),
                  pl.BlockSpec((k, tn), lambda i, j: (0, j)),
                  pl.BlockSpec((nt, tm, HEAD_DIM), lambda i, j: (0, i, 0)),
                  pl.BlockSpec(gains.shape, lambda i, j: (0, 0))],
        out_specs=pl.BlockSpec((tm, tn), lambda i, j: (i, j)),
        compiler_params=_cparams(("parallel", "arbitrary"), est),
        name="in_proj",
    )(a, b, tabs, gains)


def _band_mask(n, blk, window, seq_len):
    rows = lax.broadcasted_iota(jnp.int32, (blk, 3 * blk), 0)
    cols = lax.broadcasted_iota(jnp.int32, (blk, 3 * blk), 1)
    kpos = (n - 1) * blk + cols
    dist = jnp.abs(rows + blk - cols)
    return (dist <= window) & (kpos >= 0) & (kpos < seq_len)


def _window_kernel(blk, window, seq_len, group, sink_ref, q_ref, kp_ref, kc_ref, kn_ref,
                   vp_ref, vc_ref, vn_ref, o_ref):
    kh = pl.program_id(0)
    n = pl.program_id(1)
    mask = _band_mask(n, blk, window, seq_len)
    kcat = jnp.concatenate([kp_ref[...], kc_ref[...], kn_ref[...]], axis=0)
    vcat = jnp.concatenate([vp_ref[...], vc_ref[...], vn_ref[...]], axis=0)
    for g in range(group):
        sl = slice(g * HEAD_DIM, (g + 1) * HEAD_DIM)
        s = jnp.where(mask, _dot_nt(q_ref[:, sl], kcat), NEG_INF)
        sink = sink_ref[kh * group + g] * LOG2E
        m = jnp.maximum(jnp.max(s, axis=-1, keepdims=True), sink)
        p = jnp.exp2(s - m)
        denom = jnp.sum(p, axis=-1, keepdims=True) + jnp.exp2(sink - m)
        o = _dot(p.astype(jnp.bfloat16), vcat) / denom
        o_ref[:, sl] = o.astype(o_ref.dtype)


def _window_attention(qkv, sink, *, q_col, k_col, v_col, n_kv, group, blk, window):
    s = qkv.shape[0]
    nb = s // blk
    prev = lambda n: jnp.maximum(n - 1, 0)
    nxt = lambda n: jnp.minimum(n + 1, nb - 1)
    kv_spec = lambda col, f: pl.BlockSpec((blk, HEAD_DIM), lambda kh, n: (f(n), col + kh))
    ident = lambda n: n
    est = 2 * (2 * blk * group * HEAD_DIM * 2 + 6 * blk * HEAD_DIM * 2) + 8 * blk * 3 * blk * 4
    return pl.pallas_call(
        functools.partial(_window_kernel, blk, window, s, group),
        out_shape=jax.ShapeDtypeStruct((s, n_kv * group * HEAD_DIM), jnp.bfloat16),
        grid=(n_kv, nb),
        in_specs=[pl.BlockSpec(memory_space=pltpu.SMEM),
                  pl.BlockSpec((blk, group * HEAD_DIM), lambda kh, n: (n, q_col // group + kh)),
                  kv_spec(k_col, prev), kv_spec(k_col, ident), kv_spec(k_col, nxt),
                  kv_spec(v_col, prev), kv_spec(v_col, ident), kv_spec(v_col, nxt)],
        out_specs=pl.BlockSpec((blk, group * HEAD_DIM), lambda kh, n: (n, kh)),
        compiler_params=_cparams(("parallel", "arbitrary"), est),
        name="window_attn",
    )(sink, qkv, qkv, qkv, qkv, qkv, qkv, qkv)


def _dilated_kernel(blk, seq_len, heads, q_ref, kp_ref, kc_ref, kn_ref, vp_ref, vc_ref, vn_ref,
                    o_ref, lse_ref):
    n = pl.program_id(0)
    mask = _band_mask(n, blk, blk, seq_len)
    for h in range(heads):
        sl = slice(h * HEAD_DIM, (h + 1) * HEAD_DIM)
        kcat = jnp.concatenate([kp_ref[:, sl], kc_ref[:, sl], kn_ref[:, sl]], axis=0)
        vcat = jnp.concatenate([vp_ref[:, sl], vc_ref[:, sl], vn_ref[:, sl]], axis=0)
        s = jnp.where(mask, _dot_nt(q_ref[:, sl], kcat), NEG_INF)
        m = jnp.max(s, axis=-1, keepdims=True)
        p = jnp.exp2(s - m)
        denom = jnp.sum(p, axis=-1, keepdims=True)
        o_ref[:, sl] = _dot(p.astype(jnp.bfloat16), vcat) / denom
        lse_ref[:, sl] = jnp.broadcast_to(m + jnp.log2(denom), (blk, HEAD_DIM))


def _dilated_attention(q, k, v, *, dil, blk):
    s, width = q.shape
    heads = width // HEAD_DIM
    fold_len = s // dil
    nb = fold_len // blk
    fold = lambda t: t.reshape(fold_len, dil * width)
    spec = lambda f: pl.BlockSpec((blk, width), lambda n, r: (f(n), r))
    prev = lambda n: jnp.maximum(n - 1, 0)
    nxt = lambda n: jnp.minimum(n + 1, nb - 1)
    ident = lambda n: n
    est = 2 * (7 * blk * width * 2 + 2 * blk * width * 4) + 8 * blk * 3 * blk * 4
    qf, kf, vf = fold(q), fold(k), fold(v)
    o, lse = pl.pallas_call(
        functools.partial(_dilated_kernel, blk, fold_len, heads),
        out_shape=[jax.ShapeDtypeStruct((fold_len, dil * width), jnp.float32)] * 2,
        grid=(nb, dil),
        in_specs=[spec(ident), spec(prev), spec(ident), spec(nxt), spec(prev), spec(ident), spec(nxt)],
        out_specs=[spec(ident), spec(ident)],
        compiler_params=_cparams(("parallel", "arbitrary"), est),
        name=f"dilated_attn_{dil}",
    )(qf, kf, kf, kf, vf, vf, vf)
    return o.reshape(s, width), lse.reshape(s, width)


def _merge_kernel(n_groups, *refs):
    o_refs, l_refs, out_ref = refs[:n_groups], refs[n_groups:2 * n_groups], refs[2 * n_groups]
    lses = [r[...] for r in l_refs]
    m = functools.reduce(jnp.maximum, lses)
    ws = [jnp.exp2(l - m) for l in lses]
    tot = functools.reduce(jnp.add, ws)
    acc = functools.reduce(jnp.add, [w * r[...] for w, r in zip(ws, o_refs)])
    out_ref[...] = (acc / tot).astype(out_ref.dtype)


def _merge_groups(outs, lses, *, tm):
    s, width = outs[0].shape
    n = len(outs)
    spec = pl.BlockSpec((tm, width), lambda i: (i, 0))
    return pl.pallas_call(
        functools.partial(_merge_kernel, n),
        out_shape=jax.ShapeDtypeStruct((s, width), jnp.bfloat16),
        grid=(s // tm,),
        in_specs=[spec] * (2 * n),
        out_specs=spec,
        compiler_params=_cparams(("parallel",), 2 * (2 * n * tm * width * 4 + tm * width * 2)),
        name="dilated_merge",
    )(*outs, *lses)


ONES_ROWS = 16


def _transpose_kernel(x_ref, o_ref):
    o_ref[...] = x_ref[...].astype(jnp.float32).T.astype(o_ref.dtype)


def _transpose_cols(x, col0, ncols, *, ts, tc):
    s = x.shape[0]
    return pl.pallas_call(
        _transpose_kernel,
        out_shape=jax.ShapeDtypeStruct((ncols, s), x.dtype),
        grid=(s // ts, ncols // tc),
        in_specs=[pl.BlockSpec((ts, tc), lambda i, c: (i, col0 // tc + c))],
        out_specs=pl.BlockSpec((tc, ts), lambda i, c: (c, i)),
        compiler_params=_cparams(("parallel", "parallel"), 4 * ts * tc * 2 + 3 * ts * tc * 4),
        name="transpose_cols",
    )(x)


def _online_step_t(q, k, vt_aug, m_ref, acc_ref):
    st = _dot_nt(k, q)
    m_prev = m_ref[...]
    m_new = jnp.maximum(m_prev, jnp.max(st, axis=0, keepdims=True))
    pt = jnp.exp2(st - m_new).astype(jnp.bfloat16)
    acc_ref[...] = jnp.exp2(m_prev - m_new) * acc_ref[...] + _dot(vt_aug, pt)
    m_ref[...] = m_new


def _with_ones_rows(vt):
    return jnp.concatenate([vt, jnp.ones((ONES_ROWS, vt.shape[1]), vt.dtype)], axis=0)


def _normalised(acc_ref, dv):
    return acc_ref[:dv, :] / acc_ref[dv:dv + 1, :]


def _diff_kernel(out_scale, lam_ref, q1_ref, q2_ref, k1_ref, k2_ref, vt_ref, g_ref, o_ref, m_ref, acc_ref):
    kb = pl.program_id(2)
    dv = vt_ref.shape[0]

    @pl.when(kb == 0)
    def _():
        m_ref[...] = jnp.full(m_ref.shape, NEG_INF, jnp.float32)
        acc_ref[...] = jnp.zeros(acc_ref.shape, jnp.float32)

    vt_aug = _with_ones_rows(vt_ref[...])
    _online_step_t(q1_ref[...], k1_ref[...], vt_aug, m_ref.at[0], acc_ref.at[0])
    _online_step_t(q2_ref[...], k2_ref[...], vt_aug, m_ref.at[1], acc_ref.at[1])

    @pl.when(kb == pl.num_programs(2) - 1)
    def _():
        o = (_normalised(acc_ref.at[0], dv) - lam_ref[0] * _normalised(acc_ref.at[1], dv)).T
        ms = jnp.mean(o * o, axis=-1, keepdims=True)
        o_ref[...] = (o * lax.rsqrt(ms + RMS_EPS) * g_ref[...] * out_scale).astype(o_ref.dtype)


def _diff_attention(qkv, vt, lam, subln_g, out_scale, *, q_col, k_col, heads, tq, tk):
    s = qkv.shape[0]
    dv = 2 * HEAD_DIM
    q_spec = lambda off: pl.BlockSpec((tq, HEAD_DIM), lambda h, i, j: (i, q_col + 2 * h + off))
    k_spec = lambda off: pl.BlockSpec((tk, HEAD_DIM), lambda h, i, j: (j, k_col + 2 * h + off))
    est = (2 * (2 * tq * HEAD_DIM * 2 + 2 * tk * HEAD_DIM * 2 + tk * dv * 2 + tq * dv * 2)
           + 2 * tq * (dv + 3 * ONES_ROWS) * 4 + 6 * tq * tk * 4)
    return pl.pallas_call(
        functools.partial(_diff_kernel, out_scale),
        out_shape=jax.ShapeDtypeStruct((s, heads * dv), jnp.bfloat16),
        grid=(heads, s // tq, s // tk),
        in_specs=[pl.BlockSpec(memory_space=pltpu.SMEM),
                  q_spec(0), q_spec(1), k_spec(0), k_spec(1),
                  pl.BlockSpec((dv, tk), lambda h, i, j: (h, j)),
                  pl.BlockSpec((1, dv), lambda h, i, j: (0, 0))],
        out_specs=pl.BlockSpec((tq, dv), lambda h, i, j: (i, h)),
        scratch_shapes=[pltpu.VMEM((2, 1, tq), jnp.float32), pltpu.VMEM((2, dv + ONES_ROWS, tq), jnp.float32)],
        compiler_params=_cparams(("parallel", "parallel", "arbitrary"), est),
        name="diff_attn",
    )(lam, qkv, qkv, qkv, qkv, vt, subln_g)


def _gqa_kernel(group, q_ref, k_ref, vt_ref, o_ref, m_ref, acc_ref):
    kb = pl.program_id(2)
    tq = q_ref.shape[0]

    @pl.when(kb == 0)
    def _():
        m_ref[...] = jnp.full(m_ref.shape, NEG_INF, jnp.float32)
        acc_ref[...] = jnp.zeros(acc_ref.shape, jnp.float32)

    q = jnp.concatenate([q_ref[:, g * HEAD_DIM:(g + 1) * HEAD_DIM] for g in range(group)], axis=0)
    _online_step_t(q, k_ref[...], _with_ones_rows(vt_ref[...]), m_ref, acc_ref)

    @pl.when(kb == pl.num_programs(2) - 1)
    def _():
        o = _normalised(acc_ref, HEAD_DIM)
        for g in range(group):
            o_ref[:, g * HEAD_DIM:(g + 1) * HEAD_DIM] = o[:, g * tq:(g + 1) * tq].T.astype(o_ref.dtype)


def _gqa_attention(qkv, vt, *, q_col, k_col, n_kv, group, tq, tk):
    s = qkv.shape[0]
    lanes = group * tq
    est = (2 * (2 * tq * group * HEAD_DIM * 2 + 2 * tk * HEAD_DIM * 2)
           + lanes * (HEAD_DIM + 3 * ONES_ROWS) * 4 + 3 * lanes * tk * 4)
    return pl.pallas_call(
        functools.partial(_gqa_kernel, group),
        out_shape=jax.ShapeDtypeStruct((s, n_kv * group * HEAD_DIM), jnp.bfloat16),
        grid=(n_kv, s // tq, s // tk),
        in_specs=[pl.BlockSpec((tq, group * HEAD_DIM), lambda kh, i, j: (i, q_col // group + kh)),
                  pl.BlockSpec((tk, HEAD_DIM), lambda kh, i, j: (j, k_col + kh)),
                  pl.BlockSpec((HEAD_DIM, tk), lambda kh, i, j: (kh, j))],
        out_specs=pl.BlockSpec((tq, group * HEAD_DIM), lambda kh, i, j: (i, kh)),
        scratch_shapes=[pltpu.VMEM((1, lanes), jnp.float32),
                        pltpu.VMEM((HEAD_DIM + ONES_ROWS, lanes), jnp.float32)],
        compiler_params=_cparams(("parallel", "parallel", "arbitrary"), est),
        name="gqa_attn",
    )(qkv, qkv, vt)


def _ln_inplace(of_ref, ob_ref, g_ref, b_ref, tn):
    d = of_ref.shape[1]
    chunks = [slice(c * tn, (c + 1) * tn) for c in range(d // tn)]
    tot = functools.reduce(jnp.add, [jnp.sum(of_ref[:, sl], axis=-1, keepdims=True) for sl in chunks])
    mu = tot * (1.0 / d)
    sq = functools.reduce(jnp.add, [jnp.sum(jnp.square(of_ref[:, sl] - mu), axis=-1, keepdims=True)
                                    for sl in chunks])
    rstd = lax.rsqrt(sq * (1.0 / d) + LN_EPS)
    for sl in chunks:
        out = (of_ref[:, sl] - mu) * rstd * g_ref[:, sl] + b_ref[:, sl]
        of_ref[:, sl] = out
        ob_ref[:, sl] = out.astype(jnp.bfloat16)


def _outproj_kernel(tn, a_ref, w_ref, x_ref, g_ref, b_ref, of_ref, ob_ref):
    j = pl.program_id(1)
    d = of_ref.shape[1]
    nj = d // tn
    y = _dot(a_ref[...], w_ref[...]) + DEEPNORM_ALPHA * x_ref[...]
    for c in range(nj):
        @pl.when(j == c)
        def _(c=c):
            of_ref[:, c * tn:(c + 1) * tn] = y

    @pl.when(j == nj - 1)
    def _():
        _ln_inplace(of_ref, ob_ref, g_ref, b_ref, tn)


def _outproj_ln(a, w, x, g, b, *, tm, tn):
    s, k = a.shape
    d = w.shape[1]
    est = 2 * (tm * k * 2 + k * tn * 2 + tm * tn * 4 + tm * d * 4 + tm * d * 2) + 4 * tm * tn * 4
    row = pl.BlockSpec((tm, d), lambda i, j: (i, 0))
    vec = pl.BlockSpec((1, d), lambda i, j: (0, 0))
    return pl.pallas_call(
        functools.partial(_outproj_kernel, tn),
        out_shape=[jax.ShapeDtypeStruct((s, d), jnp.float32), jax.ShapeDtypeStruct((s, d), jnp.bfloat16)],
        grid=(s // tm, d // tn),
        in_specs=[pl.BlockSpec((tm, k), lambda i, j: (i, 0)),
                  pl.BlockSpec((k, tn), lambda i, j: (0, j)),
                  pl.BlockSpec((tm, tn), lambda i, j: (i, j)), vec, vec],
        out_specs=[row, row],
        compiler_params=_cparams(("parallel", "arbitrary"), est),
        name="out_proj_ln",
    )(a, w, x, g, b)


def _top_rows(a, k):
    vals = []
    for _ in range(k):
        m = jnp.max(a, axis=0, keepdims=True)
        vals.append(m)
        a = jnp.where(a == m, -jnp.inf, a)
    return vals


def _router_kernel(x_ref, wq_ref, keys_ref, tau_ref, e1_ref, s2_ref, e2_ref):
    half = PEER_QDIM // 2
    q = _dot(x_ref[...], wq_ref[...]).astype(jnp.bfloat16)
    s1 = _dot_nt(keys_ref[0, 0], q[:, :half])
    s2 = _dot_nt(keys_ref[0, 1], q[:, half:])
    k = PEER_TOPK
    v1 = _top_rows(s1, k + 1)
    v2 = _top_rows(s2, k + 1)
    rid = lax.broadcasted_iota(jnp.int32, (k, s1.shape[1]), 0)
    sv2 = functools.reduce(lambda acc, i: jnp.where(rid == i, v2[i], acc), range(1, k),
                           jnp.broadcast_to(v2[0], rid.shape))
    cand = jnp.concatenate([v1[a] + sv2 for a in range(k)], axis=0)
    c = _top_rows(cand, k + 1)
    c_next = jnp.maximum(c[k], jnp.maximum(v1[k] + v2[0], v1[0] + v2[k]))
    thr = 0.5 * (c[k - 1] + c_next)
    ee2 = jnp.exp(sv2 - v2[0])
    z = functools.reduce(jnp.add, [
        jnp.exp(v1[a] - v1[0]) * jnp.sum(jnp.where(sv2 >= thr - v1[a], ee2, 0.0), axis=0, keepdims=True)
        for a in range(k)])
    tau_ref[0] = thr - s1
    e1_ref[0] = jnp.exp(s1 - v1[0]) / z
    s2_ref[0] = s2
    e2_ref[0] = jnp.exp(s2 - v2[0])


def _peer_router(xb, wq, keys, *, tt):
    t, d = xb.shape
    out = jax.ShapeDtypeStruct((PEER_HEADS, PEER_NKEYS, t), jnp.float32)
    ospec = pl.BlockSpec((1, PEER_NKEYS, tt), lambda i, h: (h, 0, i))
    est = 2 * (tt * d * 2 + d * PEER_QDIM * 2 + 4 * PEER_NKEYS * tt * 4) + 24 * PEER_QDIM * tt * 4
    return pl.pallas_call(
        _router_kernel,
        out_shape=[out] * 4,
        grid=(t // tt, PEER_HEADS),
        in_specs=[pl.BlockSpec((tt, d), lambda i, h: (i, 0)),
                  pl.BlockSpec((d, PEER_QDIM), lambda i, h: (0, h)),
                  pl.BlockSpec((1, 2, PEER_NKEYS, PEER_QDIM // 2), lambda i, h: (h, 0, 0, 0))],
        out_specs=[ospec] * 4,
        compiler_params=_cparams(("parallel", "arbitrary"), est),
        name="peer_router",
    )(xb, wq, keys)


def _gelu_tanh(h):
    c = math.sqrt(2.0 / math.pi)
    return 0.5 * h * (1.0 + jnp.tanh(c * (h + 0.044715 * (h * h * h))))


def _peer_dense_kernel(xt_ref, u_ref, vt_ref, tau_ref, e1_ref, s2_ref, e2_ref, o_ref, hid_a, hid_b, p_ref):
    e = pl.program_id(1)
    n_i = u_ref.shape[0] // PEER_NKEYS

    @pl.when(e == 0)
    def _():
        o_ref[...] = jnp.zeros(o_ref.shape, jnp.float32)
        hid_b[...] = jnp.zeros(hid_b.shape, jnp.float32)

    def step(hid_w, hid_r):
        i0 = jnp.maximum(e - 1, 0) * n_i
        for il in range(n_i):
            rows = slice(il * PEER_NKEYS, (il + 1) * PEER_NKEYS)
            gate = None
            for h in range(PEER_HEADS):
                tau = tau_ref[h, pl.ds(i0 + il, 1), :]
                e1 = e1_ref[h, pl.ds(i0 + il, 1), :]
                w = jnp.where(s2_ref[h] >= tau, e2_ref[h], 0.0) * e1
                gate = w if gate is None else gate + w
            p_ref[rows, :] = (gate * _gelu_tanh(hid_r[rows, :])).astype(jnp.bfloat16)
        hid_w[...] = _dot(u_ref[...], xt_ref[...])
        o_ref[...] += _dot(vt_ref[...], p_ref[...])

    pl.when(e % 2 == 0)(functools.partial(step, hid_a, hid_b))
    pl.when(e % 2 == 1)(functools.partial(step, hid_b, hid_a))


def _peer_dense(xtb, ub, vtb, tau, e1, s2, e2, *, tt, te):
    d, t = xtb.shape
    n_e = ub.shape[0] // te
    once = pl.Buffered(1)
    rspec = pl.BlockSpec((PEER_HEADS, PEER_NKEYS, tt), lambda i, e: (0, 0, i), pipeline_mode=once)
    est = (d * tt * 2 + 2 * te * d * 2 + 2 * d * te * 2 + 4 * PEER_HEADS * PEER_NKEYS * tt * 4
           + 2 * d * tt * 4 + te * tt * 2 + 2 * te * tt * 4 + 4 * te * tt * 4)
    return pl.pallas_call(
        _peer_dense_kernel,
        out_shape=jax.ShapeDtypeStruct((d, t), jnp.float32),
        grid=(t // tt, n_e + 1),
        in_specs=[pl.BlockSpec((d, tt), lambda i, e: (0, i), pipeline_mode=once),
                  pl.BlockSpec((te, d), lambda i, e: (jnp.minimum(e, n_e - 1), 0)),
                  pl.BlockSpec((d, te), lambda i, e: (0, jnp.maximum(e - 1, 0))),
                  rspec, rspec, rspec, rspec],
        out_specs=pl.BlockSpec((d, tt), lambda i, e: (0, i)),
        scratch_shapes=[pltpu.VMEM((te, tt), jnp.float32), pltpu.VMEM((te, tt), jnp.float32),
                        pltpu.VMEM((te, tt), jnp.bfloat16)],
        compiler_params=_cparams(("parallel", "arbitrary"), est),
        name="peer_dense",
    )(xtb, ub, vtb, tau, e1, s2, e2)


def _ffn_ln_kernel(tn, x_ref, ft_ref, g_ref, b_ref, of_ref, ob_ref):
    for c in range(of_ref.shape[1] // tn):
        sl = slice(c * tn, (c + 1) * tn)
        of_ref[:, sl] = DEEPNORM_ALPHA * x_ref[:, sl] + ft_ref[sl, :].T
    _ln_inplace(of_ref, ob_ref, g_ref, b_ref, tn)


def _ffn_ln(x, ffn_t, g, b, *, tm):
    t, d = x.shape
    row = pl.BlockSpec((tm, d), lambda i: (i, 0))
    vec = pl.BlockSpec((1, d), lambda i: (0, 0))
    return pl.pallas_call(
        functools.partial(_ffn_ln_kernel, 512),
        out_shape=[jax.ShapeDtypeStruct((t, d), jnp.float32), jax.ShapeDtypeStruct((t, d), jnp.bfloat16)],
        grid=(t // tm,),
        in_specs=[row, pl.BlockSpec((d, tm), lambda i: (0, i)), vec, vec],
        out_specs=[row, row],
        compiler_params=_cparams(("parallel",), 2 * (3 * tm * d * 4 + tm * d * 2) + 3 * tm * d * 4),
        name="ffn_ln",
    )(x, ffn_t, g, b)


def _col_tiles(sizes, modes, tn):
    out, lo = [], 0
    for size, mode in zip(sizes, modes):
        assert size % tn == 0
        out.append((lo // tn, (lo + size) // tn, mode))
        lo += size
    return out


def _even_mixer(xb, w_in, sink, tabs):
    q_scale = QK_SCALE * LOG2E
    rope_q, rope_k, plain = (0, ROPE_DIMS // 2, q_scale, None), (0, ROPE_DIMS // 2, 1.0, None), (None, 0, 1.0, None)
    modes = _col_tiles(AB_SIZES, [rope_q, rope_k, plain, rope_q, rope_k, plain], 512)
    gains = jnp.ones((8, HEAD_DIM), jnp.float32)
    qkv = _project(xb, w_in, tabs, gains, modes, tm=1024 if xb.shape[0] % 1024 == 0 else xb.shape[0], tn=512)
    cols = [0]
    for sz in AB_SIZES:
        cols.append(cols[-1] + sz // HEAD_DIM)
    oa = _window_attention(qkv, sink.reshape(-1), q_col=cols[0], k_col=cols[1], v_col=cols[2],
                           n_kv=A_KV_HEADS, group=A_GROUP, blk=A_WINDOW, window=A_WINDOW)
    gw = B_GROUP_HEADS * HEAD_DIM
    outs, lses = [], []
    for g, (win, dil) in enumerate(B_PAIRS):
        take = lambda c: qkv[:, (c * HEAD_DIM + g * gw):(c * HEAD_DIM + (g + 1) * gw)]
        o, l = _dilated_attention(take(cols[3]), take(cols[4]), take(cols[5]), dil=dil, blk=win // (2 * dil))
        outs.append(o)
        lses.append(l)
    ob = _merge_groups(outs, lses, tm=512)
    return jnp.concatenate([oa, ob], axis=-1)


def _odd_mixer(xb, w_in, lam, subln_g, q_norm_g, k_norm_g, tabs, lambda_init):
    q_scale = QK_SCALE * LOG2E
    half_ax = HEAD_DIM // 4
    rope_q, rope_k, plain = (0, ROPE_DIMS // 2, q_scale, None), (0, ROPE_DIMS // 2, 1.0, None), (None, 0, 1.0, None)
    ax_q, ax_k = (1, half_ax, q_scale, 0), (1, half_ax, 1.0, 1)
    modes = _col_tiles(CD_SIZES, [rope_q, rope_k, plain, ax_q, ax_k, plain], 512)
    gains = jnp.zeros((8, HEAD_DIM), jnp.float32).at[0].set(q_norm_g).at[1].set(k_norm_g)
    qkv = _project(xb, w_in, tabs, gains, modes, tm=1024 if xb.shape[0] % 1024 == 0 else xb.shape[0], tn=512)
    cols = [0]
    for sz in CD_SIZES:
        cols.append(cols[-1] + sz // HEAD_DIM)
    s = xb.shape[0]
    ts = min(512, s)
    vct = _transpose_cols(qkv, cols[2] * HEAD_DIM, CD_SIZES[2], ts=ts, tc=512)
    vdt = _transpose_cols(qkv, cols[5] * HEAD_DIM, CD_SIZES[5], ts=ts, tc=512)
    oc = _diff_attention(qkv, vct, lam, subln_g.reshape(1, -1), 1.0 - lambda_init, q_col=cols[0], k_col=cols[1],
                         heads=C_HEADS, tq=min(1024, s), tk=min(1024, s))
    od = _gqa_attention(qkv, vdt, q_col=cols[3], k_col=cols[4], n_kv=D_KV_HEADS, group=D_GROUP,
                        tq=min(256, s), tk=min(1024, s))
    return jnp.concatenate([oc, od], axis=-1)


def _peer_ffn(xf, xb, wq, keys, u, v, g, b):
    t = xb.shape[0]
    tt = min(512, t)
    tau, e1, s2, e2 = _peer_router(xb, wq.astype(jnp.bfloat16), keys.astype(jnp.bfloat16), tt=tt)
    ffn_t = _peer_dense(xb.T, u.astype(jnp.bfloat16), v.T.astype(jnp.bfloat16), tau, e1, s2, e2, tt=tt, te=512)
    return _ffn_ln(xf, ffn_t, g.reshape(1, -1), b.reshape(1, -1), tm=min(256, t))


def kernel(x, w_in_ab, sink_a, w_out_ab, w_in_cd, lam_q1, lam_k1, lam_q2, lam_k2, subln_g, q_norm_g, k_norm_g,
           w_out_cd, ln_mix_g, ln_mix_b, peer_wq, peer_keys, peer_u, peer_v, ln_ffn_g, ln_ffn_b):
    b, s, d = x.shape
    assert b == 1 and d == D_MODEL
    bf = jnp.bfloat16
    pos = jnp.arange(s)
    rows, cols = pos // GRID_W, pos % GRID_W
    tab_partial = _rope_tables([(0, ROPE_DIMS, pos, ROPE_THETA)])
    tab_axial = _rope_tables([(0, HEAD_DIM // 2, rows, AXIAL_THETA), (HEAD_DIM // 2, HEAD_DIM // 2, cols, AXIAL_THETA)])
    xf = x[0]
    xb = xf.astype(bf)
    for layer in range(DEPTH):
        i = layer // 2
        if layer % 2 == 0:
            mix_in = _even_mixer(xb, w_in_ab[i].astype(bf), sink_a[i], tab_partial)
            w_out = w_out_ab[i]
        else:
            lambda_init = 0.8 - 0.6 * math.exp(-0.3 * layer)
            f32 = jnp.float32
            lam = (jnp.exp(jnp.sum(lam_q1[i].astype(f32) * lam_k1[i].astype(f32)))
                   - jnp.exp(jnp.sum(lam_q2[i].astype(f32) * lam_k2[i].astype(f32))) + lambda_init)
            mix_in = _odd_mixer(xb, w_in_cd[i].astype(bf), lam.reshape(1), subln_g[i], q_norm_g[i], k_norm_g[i],
                                jnp.concatenate([tab_partial, tab_axial]), lambda_init)
            w_out = w_out_cd[i]
        xf, xb = _outproj_ln(mix_in, w_out.astype(bf), xf, ln_mix_g[layer].reshape(1, -1),
                             ln_mix_b[layer].reshape(1, -1), tm=min(512, s), tn=512)
        xf, xb = _peer_ffn(xf, xb, peer_wq[layer], peer_keys[layer], peer_u[layer], peer_v[layer],
                           ln_ffn_g[layer], ln_ffn_b[layer])
    return xf[None]
```

```python
import functools
import math

import jax
import jax.numpy as jnp
from jax import lax
from jax.experimental import pallas as pl
from jax.experimental.pallas import tpu as pltpu

D_MODEL = 4096
DEPTH = 2
HEAD_DIM = 128
N_HEADS = D_MODEL // HEAD_DIM
ROPE_DIMS = HEAD_DIM // 4
ROPE_THETA = 500000.0
AXIAL_THETA = 10000.0
GRID_W = 64
A_HEADS = N_HEADS // 2
A_KV_HEADS = A_HEADS // 4
A_GROUP = A_HEADS // A_KV_HEADS
A_WINDOW = 128
B_PAIRS = ((128, 1), (512, 4), (2048, 16))
B_GROUP_HEADS = N_HEADS // 4
B_HEADS = len(B_PAIRS) * B_GROUP_HEADS
C_HEADS = N_HEADS // 4
D_HEADS = N_HEADS // 2
D_KV_HEADS = D_HEADS // 4
D_GROUP = D_HEADS // D_KV_HEADS
PEER_HEADS = 8
PEER_NKEYS = 128
PEER_EXPERTS = PEER_NKEYS * PEER_NKEYS
PEER_QDIM = 256
PEER_TOPK = 16
DEEPNORM_ALPHA = (2 * DEPTH) ** 0.25
LN_EPS = 1e-5
RMS_EPS = 1e-6
NEG_INF = -1e30
LOG2E = 1.4426950408889634
QK_SCALE = HEAD_DIM ** -0.5

AB_SIZES = (A_HEADS * HEAD_DIM, A_KV_HEADS * HEAD_DIM, A_KV_HEADS * HEAD_DIM,
            B_HEADS * HEAD_DIM, B_HEADS * HEAD_DIM, B_HEADS * HEAD_DIM)
CD_SIZES = (C_HEADS * 2 * HEAD_DIM, C_HEADS * 2 * HEAD_DIM, C_HEADS * 2 * HEAD_DIM,
            D_HEADS * HEAD_DIM, D_KV_HEADS * HEAD_DIM, D_KV_HEADS * HEAD_DIM)

LANES = 128
V7X_VMEM_BYTES = 64 * 1024 * 1024


def _vmem_limit(nbytes):
    return int(min(max(nbytes * 5 // 4 + (4 << 20), 16 << 20), V7X_VMEM_BYTES - (6 << 20)))


def _cparams(sem, nbytes):
    return pltpu.CompilerParams(dimension_semantics=sem, vmem_limit_bytes=_vmem_limit(nbytes))


def _dot(a, b):
    return jnp.dot(a, b, preferred_element_type=jnp.float32)


def _dot_nt(a, b):
    return lax.dot_general(a, b, (((1,), (1,)), ((), ())), preferred_element_type=jnp.float32)


def _rope_tables(spans):
    s = spans[0][2].shape[0]
    c = jnp.ones((s, HEAD_DIM), jnp.float32)
    sa = jnp.zeros((s, HEAD_DIM), jnp.float32)
    sb = jnp.zeros((s, HEAD_DIM), jnp.float32)
    for start, n, pos, theta in spans:
        half = n // 2
        inv = theta ** (-jnp.arange(half, dtype=jnp.float32) / half)
        ang = pos.astype(jnp.float32)[:, None] * inv[None, :]
        cos, sin = jnp.cos(ang), jnp.sin(ang)
        c = c.at[:, start:start + half].set(cos).at[:, start + half:start + n].set(cos)
        sa = sa.at[:, start:start + half].set(-sin)
        sb = sb.at[:, start + half:start + n].set(sin)
    return jnp.stack([c, sa, sb])


def _proj_kernel(col_modes, tn, a_ref, b_ref, tabs_ref, gains_ref, o_ref):
    j = pl.program_id(1)
    acc = _dot(a_ref[...], b_ref[...])

    def epilogue(mode):
        tab, half, scale, gain = mode
        for c in range(tn // HEAD_DIM):
            sl = slice(c * HEAD_DIM, (c + 1) * HEAD_DIM)
            xc = acc[:, sl]
            if gain is not None:
                ms = jnp.mean(xc * xc, axis=-1, keepdims=True)
                xc = xc * lax.rsqrt(ms + RMS_EPS) * gains_ref[gain:gain + 1, :]
            if tab is not None:
                xc = (xc * tabs_ref[3 * tab] + pltpu.roll(xc, HEAD_DIM - half, 1) * tabs_ref[3 * tab + 1]
                      + pltpu.roll(xc, half, 1) * tabs_ref[3 * tab + 2])
            if scale != 1.0:
                xc = xc * scale
            o_ref[:, sl] = xc.astype(o_ref.dtype)

    for lo, hi, mode in col_modes:
        pl.when((j >= lo) & (j < hi))(functools.partial(epilogue, mode))


def _project(a, b, tabs, gains, col_modes, *, tm, tn):
    s, k = a.shape
    n = b.shape[1]
    nt = tabs.shape[0]
    est = 2 * (tm * k * 2 + k * tn * 2 + tm * tn * 2 + nt * tm * HEAD_DIM * 4) + 3 * tm * tn * 4
    return pl.pallas_call(
        functools.partial(_proj_kernel, col_modes, tn),
        out_shape=jax.ShapeDtypeStruct((s, n), jnp.bfloat16),
        grid=(s // tm, n // tn),
        in_specs=[pl.BlockSpec((tm, k), lambda i, j: (i, 0)),
                  pl.BlockSpec((k, tn), lambda i, j: (0, j)),
                  pl.BlockSpec((nt, tm, HEAD_DIM), lambda i, j: (0, i, 0)),
                  pl.BlockSpec(gains.shape, lambda i, j: (0, 0))],
        out_specs=pl.BlockSpec((tm, tn), lambda i, j: (i, j)),
        compiler_params=_cparams(("parallel", "arbitrary"), est),
        name="in_proj",
    )(a, b, tabs, gains)


def _band_mask(n, blk, window, seq_len):
    rows = lax.broadcasted_iota(jnp.int32, (blk, 3 * blk), 0)
    cols = lax.broadcasted_iota(jnp.int32, (blk, 3 * blk), 1)
    kpos = (n - 1) * blk + cols
    dist = jnp.abs(rows + blk - cols)
    return (dist <= window) & (kpos >= 0) & (kpos < seq_len)


def _window_kernel(blk, window, seq_len, group, sink_ref, q_ref, kp_ref, kc_ref, kn_ref,
                   vp_ref, vc_ref, vn_ref, o_ref):
    kh = pl.program_id(0)
    n = pl.program_id(1)
    mask = _band_mask(n, blk, window, seq_len)
    kcat = jnp.concatenate([kp_ref[...], kc_ref[...], kn_ref[...]], axis=0)
    vcat = jnp.concatenate([vp_ref[...], vc_ref[...], vn_ref[...]], axis=0)
    for g in range(group):
        sl = slice(g * HEAD_DIM, (g + 1) * HEAD_DIM)
        s = jnp.where(mask, _dot_nt(q_ref[:, sl], kcat), NEG_INF)
        sink = sink_ref[kh * group + g] * LOG2E
        m = jnp.maximum(jnp.max(s, axis=-1, keepdims=True), sink)
        p = jnp.exp2(s - m)
        denom = jnp.sum(p, axis=-1, keepdims=True) + jnp.exp2(sink - m)
        o = _dot(p.astype(jnp.bfloat16), vcat) / denom
        o_ref[:, sl] = o.astype(o_ref.dtype)


def _window_attention(qkv, sink, *, q_col, k_col, v_col, n_kv, group, blk, window):
    s = qkv.shape[0]
    nb = s // blk
    prev = lambda n: jnp.maximum(n - 1, 0)
    nxt = lambda n: jnp.minimum(n + 1, nb - 1)
    kv_spec = lambda col, f: pl.BlockSpec((blk, HEAD_DIM), lambda kh, n: (f(n), col + kh))
    ident = lambda n: n
    est = 2 * (2 * blk * group * HEAD_DIM * 2 + 6 * blk * HEAD_DIM * 2) + 8 * blk * 3 * blk * 4
    return pl.pallas_call(
        functools.partial(_window_kernel, blk, window, s, group),
        out_shape=jax.ShapeDtypeStruct((s, n_kv * group * HEAD_DIM), jnp.bfloat16),
        grid=(n_kv, nb),
        in_specs=[pl.BlockSpec(memory_space=pltpu.SMEM),
                  pl.BlockSpec((blk, group * HEAD_DIM), lambda kh, n: (n, q_col // group + kh)),
                  kv_spec(k_col, prev), kv_spec(k_col, ident), kv_spec(k_col, nxt),
                  kv_spec(v_col, prev), kv_spec(v_col, ident), kv_spec(v_col, nxt)],
        out_specs=pl.BlockSpec((blk, group * HEAD_DIM), lambda kh, n: (n, kh)),
        compiler_params=_cparams(("parallel", "arbitrary"), est),
        name="window_attn",
    )(sink, qkv, qkv, qkv, qkv, qkv, qkv, qkv)


def _dilated_kernel(blk, seq_len, heads, q_ref, kp_ref, kc_ref, kn_ref, vp_ref, vc_ref, vn_ref,
                    o_ref, lse_ref):
    n = pl.program_id(0)
    mask = _band_mask(n, blk, blk, seq_len)
    for h in range(heads):
        sl = slice(h * HEAD_DIM, (h + 1) * HEAD_DIM)
        kcat = jnp.concatenate([kp_ref[:, sl], kc_ref[:, sl], kn_ref[:, sl]], axis=0)
        vcat = jnp.concatenate([vp_ref[:, sl], vc_ref[:, sl], vn_ref[:, sl]], axis=0)
        s = jnp.where(mask, _dot_nt(q_ref[:, sl], kcat), NEG_INF)
        m = jnp.max(s, axis=-1, keepdims=True)
        p = jnp.exp2(s - m)
        denom = jnp.sum(p, axis=-1, keepdims=True)
        o_ref[:, sl] = _dot(p.astype(jnp.bfloat16), vcat) / denom
        lse_ref[:, sl] = jnp.broadcast_to(m + jnp.log2(denom), (blk, HEAD_DIM))


def _dilated_attention(q, k, v, *, dil, blk):
    s, width = q.shape
    heads = width // HEAD_DIM
    fold_len = s // dil
    nb = fold_len // blk
    fold = lambda t: t.reshape(fold_len, dil * width)
    spec = lambda f: pl.BlockSpec((blk, width), lambda n, r: (f(n), r))
    prev = lambda n: jnp.maximum(n - 1, 0)
    nxt = lambda n: jnp.minimum(n + 1, nb - 1)
    ident = lambda n: n
    est = 2 * (7 * blk * width * 2 + 2 * blk * width * 4) + 8 * blk * 3 * blk * 4
    qf, kf, vf = fold(q), fold(k), fold(v)
    o, lse = pl.pallas_call(
        functools.partial(_dilated_kernel, blk, fold_len, heads),
        out_shape=[jax.ShapeDtypeStruct((fold_len, dil * width), jnp.float32)] * 2,
        grid=(nb, dil),
        in_specs=[spec(ident), spec(prev), spec(ident), spec(nxt), spec(prev), spec(ident), spec(nxt)],
        out_specs=[spec(ident), spec(ident)],
        compiler_params=_cparams(("parallel", "arbitrary"), est),
        name=f"dilated_attn_{dil}",
    )(qf, kf, kf, kf, vf, vf, vf)
    return o.reshape(s, width), lse.reshape(s, width)


def _merge_kernel(n_groups, *refs):
    o_refs, l_refs, out_ref = refs[:n_groups], refs[n_groups:2 * n_groups], refs[2 * n_groups]
    lses = [r[...] for r in l_refs]
    m = functools.reduce(jnp.maximum, lses)
    ws = [jnp.exp2(l - m) for l in lses]
    tot = functools.reduce(jnp.add, ws)
    acc = functools.reduce(jnp.add, [w * r[...] for w, r in zip(ws, o_refs)])
    out_ref[...] = (acc / tot).astype(out_ref.dtype)


def _merge_groups(outs, lses, *, tm):
    s, width = outs[0].shape
    n = len(outs)
    spec = pl.BlockSpec((tm, width), lambda i: (i, 0))
    return pl.pallas_call(
        functools.partial(_merge_kernel, n),
        out_shape=jax.ShapeDtypeStruct((s, width), jnp.bfloat16),
        grid=(s // tm,),
        in_specs=[spec] * (2 * n),
        out_specs=spec,
        compiler_params=_cparams(("parallel",), 2 * (2 * n * tm * width * 4 + tm * width * 2)),
        name="dilated_merge",
    )(*outs, *lses)


ONES_ROWS = 16


def _transpose_kernel(x_ref, o_ref):
    o_ref[...] = x_ref[...].astype(jnp.float32).T.astype(o_ref.dtype)


def _transpose_cols(x, col0, ncols, *, ts, tc):
    s = x.shape[0]
    return pl.pallas_call(
        _transpose_kernel,
        out_shape=jax.ShapeDtypeStruct((ncols, s), x.dtype),
        grid=(s // ts, ncols // tc),
        in_specs=[pl.BlockSpec((ts, tc), lambda i, c: (i, col0 // tc + c))],
        out_specs=pl.BlockSpec((tc, ts), lambda i, c: (c, i)),
        compiler_params=_cparams(("parallel", "parallel"), 4 * ts * tc * 2 + 3 * ts * tc * 4),
        name="transpose_cols",
    )(x)


def _online_step_t(q, k, vt_aug, m_ref, acc_ref):
    st = _dot_nt(k, q)
    m_prev = m_ref[...]
    m_new = jnp.maximum(m_prev, jnp.max(st, axis=0, keepdims=True))
    pt = jnp.exp2(st - m_new).astype(jnp.bfloat16)
    acc_ref[...] = jnp.exp2(m_prev - m_new) * acc_ref[...] + _dot(vt_aug, pt)
    m_ref[...] = m_new


def _with_ones_rows(vt):
    return jnp.concatenate([vt, jnp.ones((ONES_ROWS, vt.shape[1]), vt.dtype)], axis=0)


def _normalised(acc_ref, dv):
    return acc_ref[:dv, :] / acc_ref[dv:dv + 1, :]


def _diff_kernel(out_scale, lam_ref, q1_ref, q2_ref, k1_ref, k2_ref, vt_ref, g_ref, o_ref, m_ref, acc_ref):
    kb = pl.program_id(2)
    dv = vt_ref.shape[0]

    @pl.when(kb == 0)
    def _():
        m_ref[...] = jnp.full(m_ref.shape, NEG_INF, jnp.float32)
        acc_ref[...] = jnp.zeros(acc_ref.shape, jnp.float32)

    vt_aug = _with_ones_rows(vt_ref[...])
    _online_step_t(q1_ref[...], k1_ref[...], vt_aug, m_ref.at[0], acc_ref.at[0])
    _online_step_t(q2_ref[...], k2_ref[...], vt_aug, m_ref.at[1], acc_ref.at[1])

    @pl.when(kb == pl.num_programs(2) - 1)
    def _():
        o = (_normalised(acc_ref.at[0], dv) - lam_ref[0] * _normalised(acc_ref.at[1], dv)).T
        ms = jnp.mean(o * o, axis=-1, keepdims=True)
        o_ref[...] = (o * lax.rsqrt(ms + RMS_EPS) * g_ref[...] * out_scale).astype(o_ref.dtype)


def _diff_attention(qkv, vt, lam, subln_g, out_scale, *, q_col, k_col, heads, tq, tk):
    s = qkv.shape[0]
    dv = 2 * HEAD_DIM
    q_spec = lambda off: pl.BlockSpec((tq, HEAD_DIM), lambda h, i, j: (i, q_col + 2 * h + off))
    k_spec = lambda off: pl.BlockSpec((tk, HEAD_DIM), lambda h, i, j: (j, k_col + 2 * h + off))
    est = (2 * (2 * tq * HEAD_DIM * 2 + 2 * tk * HEAD_DIM * 2 + tk * dv * 2 + tq * dv * 2)
           + 2 * tq * (dv + 3 * ONES_ROWS) * 4 + 6 * tq * tk * 4)
    return pl.pallas_call(
        functools.partial(_diff_kernel, out_scale),
        out_shape=jax.ShapeDtypeStruct((s, heads * dv), jnp.bfloat16),
        grid=(heads, s // tq, s // tk),
        in_specs=[pl.BlockSpec(memory_space=pltpu.SMEM),
                  q_spec(0), q_spec(1), k_spec(0), k_spec(1),
                  pl.BlockSpec((dv, tk), lambda h, i, j: (h, j)),
                  pl.BlockSpec((1, dv), lambda h, i, j: (0, 0))],
        out_specs=pl.BlockSpec((tq, dv), lambda h, i, j: (i, h)),
        scratch_shapes=[pltpu.VMEM((2, 1, tq), jnp.float32), pltpu.VMEM((2, dv + ONES_ROWS, tq), jnp.float32)],
        compiler_params=_cparams(("parallel", "parallel", "arbitrary"), est),
        name="diff_attn",
    )(lam, qkv, qkv, qkv, qkv, vt, subln_g)


def _gqa_kernel(group, q_ref, k_ref, vt_ref, o_ref, m_ref, acc_ref):
    kb = pl.program_id(2)
    tq = q_ref.shape[0]

    @pl.when(kb == 0)
    def _():
        m_ref[...] = jnp.full(m_ref.shape, NEG_INF, jnp.float32)
        acc_ref[...] = jnp.zeros(acc_ref.shape, jnp.float32)

    q = jnp.concatenate([q_ref[:, g * HEAD_DIM:(g + 1) * HEAD_DIM] for g in range(group)], axis=0)
    _online_step_t(q, k_ref[...], _with_ones_rows(vt_ref[...]), m_ref, acc_ref)

    @pl.when(kb == pl.num_programs(2) - 1)
    def _():
        o = _normalised(acc_ref, HEAD_DIM)
        for g in range(group):
            o_ref[:, g * HEAD_DIM:(g + 1) * HEAD_DIM] = o[:, g * tq:(g + 1) * tq].T.astype(o_ref.dtype)


def _gqa_attention(qkv, vt, *, q_col, k_col, n_kv, group, tq, tk):
    s = qkv.shape[0]
    lanes = group * tq
    est = (2 * (2 * tq * group * HEAD_DIM * 2 + 2 * tk * HEAD_DIM * 2)
           + lanes * (HEAD_DIM + 3 * ONES_ROWS) * 4 + 3 * lanes * tk * 4)
    return pl.pallas_call(
        functools.partial(_gqa_kernel, group),
        out_shape=jax.ShapeDtypeStruct((s, n_kv * group * HEAD_DIM), jnp.bfloat16),
        grid=(n_kv, s // tq, s // tk),
        in_specs=[pl.BlockSpec((tq, group * HEAD_DIM), lambda kh, i, j: (i, q_col // group + kh)),
                  pl.BlockSpec((tk, HEAD_DIM), lambda kh, i, j: (j, k_col + kh)),
                  pl.BlockSpec((HEAD_DIM, tk), lambda kh, i, j: (kh, j))],
        out_specs=pl.BlockSpec((tq, group * HEAD_DIM), lambda kh, i, j: (i, kh)),
        scratch_shapes=[pltpu.VMEM((1, lanes), jnp.float32),
                        pltpu.VMEM((HEAD_DIM + ONES_ROWS, lanes), jnp.float32)],
        compiler_params=_cparams(("parallel", "parallel", "arbitrary"), est),
        name="gqa_attn",
    )(qkv, qkv, vt)


def _ln_inplace(of_ref, ob_ref, g_ref, b_ref, tn):
    d = of_ref.shape[1]
    chunks = [slice(c * tn, (c + 1) * tn) for c in range(d // tn)]
    tot = functools.reduce(jnp.add, [jnp.sum(of_ref[:, sl], axis=-1, keepdims=True) for sl in chunks])
    mu = tot * (1.0 / d)
    sq = functools.reduce(jnp.add, [jnp.sum(jnp.square(of_ref[:, sl] - mu), axis=-1, keepdims=True)
                                    for sl in chunks])
    rstd = lax.rsqrt(sq * (1.0 / d) + LN_EPS)
    for sl in chunks:
        out = (of_ref[:, sl] - mu) * rstd * g_ref[:, sl] + b_ref[:, sl]
        of_ref[:, sl] = out
        ob_ref[:, sl] = out.astype(jnp.bfloat16)


def _outproj_kernel(tn, a_ref, w_ref, x_ref, g_ref, b_ref, of_ref, ob_ref):
    j = pl.program_id(1)
    d = of_ref.shape[1]
    nj = d // tn
    y = _dot(a_ref[...], w_ref[...]) + DEEPNORM_ALPHA * x_ref[...]
    for c in range(nj):
        @pl.when(j == c)
        def _(c=c):
            of_ref[:, c * tn:(c + 1) * tn] = y

    @pl.when(j == nj - 1)
    def _():
        _ln_inplace(of_ref, ob_ref, g_ref, b_ref, tn)


def _outproj_ln(a, w, x, g, b, *, tm, tn):
    s, k = a.shape
    d = w.shape[1]
    est = 2 * (tm * k * 2 + k * tn * 2 + tm * tn * 4 + tm * d * 4 + tm * d * 2) + 4 * tm * tn * 4
    row = pl.BlockSpec((tm, d), lambda i, j: (i, 0))
    vec = pl.BlockSpec((1, d), lambda i, j: (0, 0))
    return pl.pallas_call(
        functools.partial(_outproj_kernel, tn),
        out_shape=[jax.ShapeDtypeStruct((s, d), jnp.float32), jax.ShapeDtypeStruct((s, d), jnp.bfloat16)],
        grid=(s // tm, d // tn),
        in_specs=[pl.BlockSpec((tm, k), lambda i, j: (i, 0)),
                  pl.BlockSpec((k, tn), lambda i, j: (0, j)),
                  pl.BlockSpec((tm, tn), lambda i, j: (i, j)), vec, vec],
        out_specs=[row, row],
        compiler_params=_cparams(("parallel", "arbitrary"), est),
        name="out_proj_ln",
    )(a, w, x, g, b)


def _top_rows(a, k):
    vals = []
    for _ in range(k):
        m = jnp.max(a, axis=0, keepdims=True)
        vals.append(m)
        a = jnp.where(a == m, -jnp.inf, a)
    return vals


def _router_kernel(x_ref, wq_ref, keys_ref, tau_ref, e1_ref, s2_ref, e2_ref):
    half = PEER_QDIM // 2
    q = _dot(x_ref[...], wq_ref[...]).astype(jnp.bfloat16)
    s1 = _dot_nt(keys_ref[0, 0], q[:, :half])
    s2 = _dot_nt(keys_ref[0, 1], q[:, half:])
    k = PEER_TOPK
    v1 = _top_rows(s1, k + 1)
    v2 = _top_rows(s2, k + 1)
    rid = lax.broadcasted_iota(jnp.int32, (k, s1.shape[1]), 0)
    sv2 = functools.reduce(lambda acc, i: jnp.where(rid == i, v2[i], acc), range(1, k),
                           jnp.broadcast_to(v2[0], rid.shape))
    cand = jnp.concatenate([v1[a] + sv2 for a in range(k)], axis=0)
    c = _top_rows(cand, k + 1)
    c_next = jnp.maximum(c[k], jnp.maximum(v1[k] + v2[0], v1[0] + v2[k]))
    thr = 0.5 * (c[k - 1] + c_next)
    ee2 = jnp.exp(sv2 - v2[0])
    z = functools.reduce(jnp.add, [
        jnp.exp(v1[a] - v1[0]) * jnp.sum(jnp.where(sv2 >= thr - v1[a], ee2, 0.0), axis=0, keepdims=True)
        for a in range(k)])
    tau_ref[0] = thr - s1
    e1_ref[0] = jnp.exp(s1 - v1[0]) / z
    s2_ref[0] = s2
    e2_ref[0] = jnp.exp(s2 - v2[0])


def _peer_router(xb, wq, keys, *, tt):
    t, d = xb.shape
    out = jax.ShapeDtypeStruct((PEER_HEADS, PEER_NKEYS, t), jnp.float32)
    ospec = pl.BlockSpec((1, PEER_NKEYS, tt), lambda i, h: (h, 0, i))
    est = 2 * (tt * d * 2 + d * PEER_QDIM * 2 + 4 * PEER_NKEYS * tt * 4) + 24 * PEER_QDIM * tt * 4
    return pl.pallas_call(
        _router_kernel,
        out_shape=[out] * 4,
        grid=(t // tt, PEER_HEADS),
        in_specs=[pl.BlockSpec((tt, d), lambda i, h: (i, 0)),
                  pl.BlockSpec((d, PEER_QDIM), lambda i, h: (0, h)),
                  pl.BlockSpec((1, 2, PEER_NKEYS, PEER_QDIM // 2), lambda i, h: (h, 0, 0, 0))],
        out_specs=[ospec] * 4,
        compiler_params=_cparams(("parallel", "arbitrary"), est),
        name="peer_router",
    )(xb, wq, keys)


def _gelu_tanh(h):
    c = math.sqrt(2.0 / math.pi)
    return 0.5 * h * (1.0 + jnp.tanh(c * (h + 0.044715 * (h * h * h))))


def _peer_dense_kernel(xt_ref, u_ref, vt_ref, tau_ref, e1_ref, s2_ref, e2_ref, o_ref, hid_a, hid_b, p_ref):
    e = pl.program_id(1)
    n_i = u_ref.shape[0] // PEER_NKEYS

    @pl.when(e == 0)
    def _():
        o_ref[...] = jnp.zeros(o_ref.shape, jnp.float32)
        hid_b[...] = jnp.zeros(hid_b.shape, jnp.float32)

    def step(hid_w, hid_r):
        i0 = jnp.maximum(e - 1, 0) * n_i
        for il in range(n_i):
            rows = slice(il * PEER_NKEYS, (il + 1) * PEER_NKEYS)
            gate = None
            for h in range(PEER_HEADS):
                tau = tau_ref[h, pl.ds(i0 + il, 1), :]
                e1 = e1_ref[h, pl.ds(i0 + il, 1), :]
                w = jnp.where(s2_ref[h] >= tau, e2_ref[h], 0.0) * e1
                gate = w if gate is None else gate + w
            p_ref[rows, :] = (gate * _gelu_tanh(hid_r[rows, :])).astype(jnp.bfloat16)
        hid_w[...] = _dot(u_ref[...], xt_ref[...])
        o_ref[...] += _dot(vt_ref[...], p_ref[...])

    pl.when(e % 2 == 0)(functools.partial(step, hid_a, hid_b))
    pl.when(e % 2 == 1)(functools.partial(step, hid_b, hid_a))


def _peer_dense(xtb, ub, vtb, tau, e1, s2, e2, *, tt, te):
    d, t = xtb.shape
    n_e = ub.shape[0] // te
    once = pl.Buffered(1)
    rspec = pl.BlockSpec((PEER_HEADS, PEER_NKEYS, tt), lambda i, e: (0, 0, i), pipeline_mode=once)
    est = (d * tt * 2 + 2 * te * d * 2 + 2 * d * te * 2 + 4 * PEER_HEADS * PEER_NKEYS * tt * 4
           + 2 * d * tt * 4 + te * tt * 2 + 2 * te * tt * 4 + 4 * te * tt * 4)
    return pl.pallas_call(
        _peer_dense_kernel,
        out_shape=jax.ShapeDtypeStruct((d, t), jnp.float32),
        grid=(t // tt, n_e + 1),
        in_specs=[pl.BlockSpec((d, tt), lambda i, e: (0, i), pipeline_mode=once),
                  pl.BlockSpec((te, d), lambda i, e: (jnp.minimum(e, n_e - 1), 0)),
                  pl.BlockSpec((d, te), lambda i, e: (0, jnp.maximum(e - 1, 0))),
                  rspec, rspec, rspec, rspec],
        out_specs=pl.BlockSpec((d, tt), lambda i, e: (0, i)),
        scratch_shapes=[pltpu.VMEM((te, tt), jnp.float32), pltpu.VMEM((te, tt), jnp.float32),
                        pltpu.VMEM((te, tt), jnp.bfloat16)],
        compiler_params=_cparams(("parallel", "arbitrary"), est),
        name="peer_dense",
    )(xtb, ub, vtb, tau, e1, s2, e2)


def _ffn_ln_kernel(tn, x_ref, ft_ref, g_ref, b_ref, of_ref, ob_ref):
    for c in range(of_ref.shape[1] // tn):
        sl = slice(c * tn, (c + 1) * tn)
        of_ref[:, sl] = DEEPNORM_ALPHA * x_ref[:, sl] + ft_ref[sl, :].T
    _ln_inplace(of_ref, ob_ref, g_ref, b_ref, tn)


def _ffn_ln(x, ffn_t, g, b, *, tm):
    t, d = x.shape
    row = pl.BlockSpec((tm, d), lambda i: (i, 0))
    vec = pl.BlockSpec((1, d), lambda i: (0, 0))
    return pl.pallas_call(
        functools.partial(_ffn_ln_kernel, 512),
        out_shape=[jax.ShapeDtypeStruct((t, d), jnp.float32), jax.ShapeDtypeStruct((t, d), jnp.bfloat16)],
        grid=(t // tm,),
        in_specs=[row, pl.BlockSpec((d, tm), lambda i: (0, i)), vec, vec],
        out_specs=[row, row],
        compiler_params=_cparams(("parallel",), 2 * (3 * tm * d * 4 + tm * d * 2) + 3 * tm * d * 4),
        name="ffn_ln",
    )(x, ffn_t, g, b)


def _col_tiles(sizes, modes, tn):
    out, lo = [], 0
    for size, mode in zip(sizes, modes):
        assert size % tn == 0
        out.append((lo // tn, (lo + size) // tn, mode))
        lo += size
    return out


def _even_mixer(xb, w_in, sink, tabs):
    q_scale = QK_SCALE * LOG2E
    rope_q, rope_k, plain = (0, ROPE_DIMS // 2, q_scale, None), (0, ROPE_DIMS // 2, 1.0, None), (None, 0, 1.0, None)
    modes = _col_tiles(AB_SIZES, [rope_q, rope_k, plain, rope_q, rope_k, plain], 512)
    gains = jnp.ones((8, HEAD_DIM), jnp.float32)
    qkv = _project(xb, w_in, tabs, gains, modes, tm=1024 if xb.shape[0] % 1024 == 0 else xb.shape[0], tn=512)
    cols = [0]
    for sz in AB_SIZES:
        cols.append(cols[-1] + sz // HEAD_DIM)
    oa = _window_attention(qkv, sink.reshape(-1), q_col=cols[0], k_col=cols[1], v_col=cols[2],
                           n_kv=A_KV_HEADS, group=A_GROUP, blk=A_WINDOW, window=A_WINDOW)
    gw = B_GROUP_HEADS * HEAD_DIM
    outs, lses = [], []
    for g, (win, dil) in enumerate(B_PAIRS):
        take = lambda c: qkv[:, (c * HEAD_DIM + g * gw):(c * HEAD_DIM + (g + 1) * gw)]
        o, l = _dilated_attention(take(cols[3]), take(cols[4]), take(cols[5]), dil=dil, blk=win // (2 * dil))
        outs.append(o)
        lses.append(l)
    ob = _merge_groups(outs, lses, tm=512)
    return jnp.concatenate([oa, ob], axis=-1)


def _odd_mixer(xb, w_in, lam, subln_g, q_norm_g, k_norm_g, tabs, lambda_init):
    q_scale = QK_SCALE * LOG2E
    half_ax = HEAD_DIM // 4
    rope_q, rope_k, plain = (0, ROPE_DIMS // 2, q_scale, None), (0, ROPE_DIMS // 2, 1.0, None), (None, 0, 1.0, None)
    ax_q, ax_k = (1, half_ax, q_scale, 0), (1, half_ax, 1.0, 1)
    modes = _col_tiles(CD_SIZES, [rope_q, rope_k, plain, ax_q, ax_k, plain], 512)
    gains = jnp.zeros((8, HEAD_DIM), jnp.float32).at[0].set(q_norm_g).at[1].set(k_norm_g)
    qkv = _project(xb, w_in, tabs, gains, modes, tm=1024 if xb.shape[0] % 1024 == 0 else xb.shape[0], tn=512)
    cols = [0]
    for sz in CD_SIZES:
        cols.append(cols[-1] + sz // HEAD_DIM)
    s = xb.shape[0]
    ts = min(512, s)
    vct = _transpose_cols(qkv, cols[2] * HEAD_DIM, CD_SIZES[2], ts=ts, tc=512)
    vdt = _transpose_cols(qkv, cols[5] * HEAD_DIM, CD_SIZES[5], ts=ts, tc=512)
    oc = _diff_attention(qkv, vct, lam, subln_g.reshape(1, -1), 1.0 - lambda_init, q_col=cols[0], k_col=cols[1],
                         heads=C_HEADS, tq=min(1024, s), tk=min(2048, s))
    od = _gqa_attention(qkv, vdt, q_col=cols[3], k_col=cols[4], n_kv=D_KV_HEADS, group=D_GROUP,
                        tq=min(256, s), tk=min(4096, s))
    return jnp.concatenate([oc, od], axis=-1)


def _peer_ffn(xf, xb, wq, keys, u, v, g, b):
    t = xb.shape[0]
    tt = min(512, t)
    tau, e1, s2, e2 = _peer_router(xb, wq.astype(jnp.bfloat16), keys.astype(jnp.bfloat16), tt=tt)
    ffn_t = _peer_dense(xb.T, u.astype(jnp.bfloat16), v.T.astype(jnp.bfloat16), tau, e1, s2, e2, tt=tt, te=512)
    return _ffn_ln(xf, ffn_t, g.reshape(1, -1), b.reshape(1, -1), tm=min(256, t))


def kernel(x, w_in_ab, sink_a, w_out_ab, w_in_cd, lam_q1, lam_k1, lam_q2, lam_k2, subln_g, q_norm_g, k_norm_g,
           w_out_cd, ln_mix_g, ln_mix_b, peer_wq, peer_keys, peer_u, peer_v, ln_ffn_g, ln_ffn_b):
    b, s, d = x.shape
    assert b == 1 and d == D_MODEL
    bf = jnp.bfloat16
    pos = jnp.arange(s)
    rows, cols = pos // GRID_W, pos % GRID_W
    tab_partial = _rope_tables([(0, ROPE_DIMS, pos, ROPE_THETA)])
    tab_axial = _rope_tables([(0, HEAD_DIM // 2, rows, AXIAL_THETA), (HEAD_DIM // 2, HEAD_DIM // 2, cols, AXIAL_THETA)])
    xf = x[0]
    xb = xf.astype(bf)
    for layer in range(DEPTH):
        i = layer // 2
        if layer % 2 == 0:
            mix_in = _even_mixer(xb, w_in_ab[i].astype(bf), sink_a[i], tab_partial)
            w_out = w_out_ab[i]
        else:
            lambda_init = 0.8 - 0.6 * math.exp(-0.3 * layer)
            f32 = jnp.float32
            lam = (jnp.exp(jnp.sum(lam_q1[i].astype(f32) * lam_k1[i].astype(f32)))
                   - jnp.exp(jnp.sum(lam_q2[i].astype(f32) * lam_k2[i].astype(f32))) + lambda_init)
            mix_in = _odd_mixer(xb, w_in_cd[i].astype(bf), lam.reshape(1), subln_g[i], q_norm_g[i], k_norm_g[i],
                                jnp.concatenate([tab_partial, tab_axial]), lambda_init)
            w_out = w_out_cd[i]
        xf, xb = _outproj_ln(mix_in, w_out.astype(bf), xf, ln_mix_g[layer].reshape(1, -1),
                             ln_mix_b[layer].reshape(1, -1), tm=min(512, s), tn=512)
        xf, xb = _peer_ffn(xf, xb, peer_wq[layer], peer_keys[layer], peer_u[layer], peer_v[layer],
                           ln_ffn_g[layer], ln_ffn_b[layer])
    return xf[None]
```

```python
import functools
import math

import jax
import jax.numpy as jnp
from jax import lax
from jax.experimental import pallas as pl
from jax.experimental.pallas import tpu as pltpu

D_MODEL = 4096
DEPTH = 2
HEAD_DIM = 128
N_HEADS = D_MODEL // HEAD_DIM
ROPE_DIMS = HEAD_DIM // 4
ROPE_THETA = 500000.0
AXIAL_THETA = 10000.0
GRID_W = 64
A_HEADS = N_HEADS // 2
A_KV_HEADS = A_HEADS // 4
A_GROUP = A_HEADS // A_KV_HEADS
A_WINDOW = 128
B_PAIRS = ((128, 1), (512, 4), (2048, 16))
B_GROUP_HEADS = N_HEADS // 4
B_HEADS = len(B_PAIRS) * B_GROUP_HEADS
C_HEADS = N_HEADS // 4
D_HEADS = N_HEADS // 2
D_KV_HEADS = D_HEADS // 4
D_GROUP = D_HEADS // D_KV_HEADS
PEER_HEADS = 8
PEER_NKEYS = 128
PEER_EXPERTS = PEER_NKEYS * PEER_NKEYS
PEER_QDIM = 256
PEER_TOPK = 16
DEEPNORM_ALPHA = (2 * DEPTH) ** 0.25
LN_EPS = 1e-5
RMS_EPS = 1e-6
NEG_INF = -1e30
LOG2E = 1.4426950408889634
QK_SCALE = HEAD_DIM ** -0.5

AB_SIZES = (A_HEADS * HEAD_DIM, A_KV_HEADS * HEAD_DIM, A_KV_HEADS * HEAD_DIM,
            B_HEADS * HEAD_DIM, B_HEADS * HEAD_DIM, B_HEADS * HEAD_DIM)
CD_SIZES = (C_HEADS * 2 * HEAD_DIM, C_HEADS * 2 * HEAD_DIM, C_HEADS * 2 * HEAD_DIM,
            D_HEADS * HEAD_DIM, D_KV_HEADS * HEAD_DIM, D_KV_HEADS * HEAD_DIM)

LANES = 128
V7X_VMEM_BYTES = 64 * 1024 * 1024


def _vmem_limit(nbytes):
    return int(min(max(nbytes * 5 // 4 + (4 << 20), 16 << 20), V7X_VMEM_BYTES - (6 << 20)))


def _cparams(sem, nbytes):
    return pltpu.CompilerParams(dimension_semantics=sem, vmem_limit_bytes=_vmem_limit(nbytes))


def _dot(a, b):
    return jnp.dot(a, b, preferred_element_type=jnp.float32)


def _dot_nt(a, b):
    return lax.dot_general(a, b, (((1,), (1,)), ((), ())), preferred_element_type=jnp.float32)


def _rope_tables(spans):
    s = spans[0][2].shape[0]
    lane = jnp.arange(HEAD_DIM)
    ang = jnp.zeros((s, HEAD_DIM), jnp.float32)
    lo = jnp.zeros((HEAD_DIM,), bool)
    hi = jnp.zeros((HEAD_DIM,), bool)
    for start, n, pos, theta in spans:
        half = n // 2
        inv = theta ** (-jnp.arange(half, dtype=jnp.float32) / half)
        in_span = (lane >= start) & (lane < start + n)
        inv_lane = inv[(lane - start) % half]
        ang = jnp.where(in_span[None, :], pos.astype(jnp.float32)[:, None] * inv_lane[None, :], ang)
        lo = lo | (in_span & (lane < start + half))
        hi = hi | (in_span & (lane >= start + half))
    cos, sin = jnp.cos(ang), jnp.sin(ang)
    c = jnp.where((lo | hi)[None, :], cos, 1.0)
    sa = jnp.where(lo[None, :], -sin, 0.0)
    sb = jnp.where(hi[None, :], sin, 0.0)
    return jnp.stack([c, sa, sb])


def _proj_kernel(col_modes, tn, a_ref, b_ref, tabs_ref, gains_ref, o_ref):
    j = pl.program_id(1)
    acc = _dot(a_ref[...], b_ref[...])

    def epilogue(mode):
        tab, half, scale, gain = mode
        for c in range(tn // HEAD_DIM):
            sl = slice(c * HEAD_DIM, (c + 1) * HEAD_DIM)
            xc = acc[:, sl]
            if gain is not None:
                ms = jnp.mean(xc * xc, axis=-1, keepdims=True)
                xc = xc * lax.rsqrt(ms + RMS_EPS) * gains_ref[gain:gain + 1, :]
            if tab is not None:
                xc = (xc * tabs_ref[3 * tab] + pltpu.roll(xc, HEAD_DIM - half, 1) * tabs_ref[3 * tab + 1]
                      + pltpu.roll(xc, half, 1) * tabs_ref[3 * tab + 2])
            if scale != 1.0:
                xc = xc * scale
            o_ref[:, sl] = xc.astype(o_ref.dtype)

    for lo, hi, mode in col_modes:
        pl.when((j >= lo) & (j < hi))(functools.partial(epilogue, mode))


def _project(a, b, tabs, gains, col_modes, *, tm, tn):
    s, k = a.shape
    n = b.shape[1]
    nt = tabs.shape[0]
    est = 2 * (tm * k * 2 + k * tn * 2 + tm * tn * 2 + nt * tm * HEAD_DIM * 4) + 3 * tm * tn * 4
    return pl.pallas_call(
        functools.partial(_proj_kernel, col_modes, tn),
        out_shape=jax.ShapeDtypeStruct((s, n), jnp.bfloat16),
        grid=(s // tm, n // tn),
        in_specs=[pl.BlockSpec((tm, k), lambda i, j: (i, 0)),
                  pl.BlockSpec((k, tn), lambda i, j: (0, j)),
                  pl.BlockSpec((nt, tm, HEAD_DIM), lambda i, j: (0, i, 0)),
                  pl.BlockSpec(gains.shape, lambda i, j: (0, 0))],
        out_specs=pl.BlockSpec((tm, tn), lambda i, j: (i, j)),
        compiler_params=_cparams(("parallel", "arbitrary"), est),
        name="in_proj",
    )(a, b, tabs, gains)


def _band_mask(n, blk, window, seq_len):
    rows = lax.broadcasted_iota(jnp.int32, (blk, 3 * blk), 0)
    cols = lax.broadcasted_iota(jnp.int32, (blk, 3 * blk), 1)
    kpos = (n - 1) * blk + cols
    dist = jnp.abs(rows + blk - cols)
    return (dist <= window) & (kpos >= 0) & (kpos < seq_len)


def _window_kernel(blk, window, seq_len, group, sink_ref, q_ref, kp_ref, kc_ref, kn_ref,
                   vp_ref, vc_ref, vn_ref, o_ref):
    kh = pl.program_id(0)
    n = pl.program_id(1)
    mask = _band_mask(n, blk, window, seq_len)
    kcat = jnp.concatenate([kp_ref[...], kc_ref[...], kn_ref[...]], axis=0)
    vcat = jnp.concatenate([vp_ref[...], vc_ref[...], vn_ref[...]], axis=0)
    for g in range(group):
        sl = slice(g * HEAD_DIM, (g + 1) * HEAD_DIM)
        s = jnp.where(mask, _dot_nt(q_ref[:, sl], kcat), NEG_INF)
        sink = sink_ref[kh * group + g] * LOG2E
        m = jnp.maximum(jnp.max(s, axis=-1, keepdims=True), sink)
        p = jnp.exp2(s - m)
        denom = jnp.sum(p, axis=-1, keepdims=True) + jnp.exp2(sink - m)
        o = _dot(p.astype(jnp.bfloat16), vcat) / denom
        o_ref[:, sl] = o.astype(o_ref.dtype)


def _window_attention(qkv, sink, *, q_col, k_col, v_col, n_kv, group, blk, window):
    s = qkv.shape[0]
    nb = s // blk
    prev = lambda n: jnp.maximum(n - 1, 0)
    nxt = lambda n: jnp.minimum(n + 1, nb - 1)
    kv_spec = lambda col, f: pl.BlockSpec((blk, HEAD_DIM), lambda kh, n: (f(n), col + kh))
    ident = lambda n: n
    est = 2 * (2 * blk * group * HEAD_DIM * 2 + 6 * blk * HEAD_DIM * 2) + 8 * blk * 3 * blk * 4
    return pl.pallas_call(
        functools.partial(_window_kernel, blk, window, s, group),
        out_shape=jax.ShapeDtypeStruct((s, n_kv * group * HEAD_DIM), jnp.bfloat16),
        grid=(n_kv, nb),
        in_specs=[pl.BlockSpec(memory_space=pltpu.SMEM),
                  pl.BlockSpec((blk, group * HEAD_DIM), lambda kh, n: (n, q_col // group + kh)),
                  kv_spec(k_col, prev), kv_spec(k_col, ident), kv_spec(k_col, nxt),
                  kv_spec(v_col, prev), kv_spec(v_col, ident), kv_spec(v_col, nxt)],
        out_specs=pl.BlockSpec((blk, group * HEAD_DIM), lambda kh, n: (n, kh)),
        compiler_params=_cparams(("parallel", "arbitrary"), est),
        name="window_attn",
    )(sink, qkv, qkv, qkv, qkv, qkv, qkv, qkv)


def _dilated_kernel(blk, seq_len, heads, q_ref, kp_ref, kc_ref, kn_ref, vp_ref, vc_ref, vn_ref,
                    o_ref, lse_ref):
    n = pl.program_id(0)
    mask = _band_mask(n, blk, blk, seq_len)
    for h in range(heads):
        sl = slice(h * HEAD_DIM, (h + 1) * HEAD_DIM)
        kcat = jnp.concatenate([kp_ref[:, sl], kc_ref[:, sl], kn_ref[:, sl]], axis=0)
        vcat = jnp.concatenate([vp_ref[:, sl], vc_ref[:, sl], vn_ref[:, sl]], axis=0)
        s = jnp.where(mask, _dot_nt(q_ref[:, sl], kcat), NEG_INF)
        m = jnp.max(s, axis=-1, keepdims=True)
        p = jnp.exp2(s - m)
        denom = jnp.sum(p, axis=-1, keepdims=True)
        o_ref[:, sl] = _dot(p.astype(jnp.bfloat16), vcat) / denom
        lse_ref[:, sl] = jnp.broadcast_to(m + jnp.log2(denom), (blk, HEAD_DIM))


def _dilated_attention(q, k, v, *, dil, blk):
    s, width = q.shape
    heads = width // HEAD_DIM
    fold_len = s // dil
    nb = fold_len // blk
    fold = lambda t: t.reshape(fold_len, dil * width)
    spec = lambda f: pl.BlockSpec((blk, width), lambda n, r: (f(n), r))
    prev = lambda n: jnp.maximum(n - 1, 0)
    nxt = lambda n: jnp.minimum(n + 1, nb - 1)
    ident = lambda n: n
    est = 2 * (7 * blk * width * 2 + 2 * blk * width * 4) + 8 * blk * 3 * blk * 4
    qf, kf, vf = fold(q), fold(k), fold(v)
    o, lse = pl.pallas_call(
        functools.partial(_dilated_kernel, blk, fold_len, heads),
        out_shape=[jax.ShapeDtypeStruct((fold_len, dil * width), jnp.float32)] * 2,
        grid=(nb, dil),
        in_specs=[spec(ident), spec(prev), spec(ident), spec(nxt), spec(prev), spec(ident), spec(nxt)],
        out_specs=[spec(ident), spec(ident)],
        compiler_params=_cparams(("parallel", "arbitrary"), est),
        name=f"dilated_attn_{dil}",
    )(qf, kf, kf, kf, vf, vf, vf)
    return o.reshape(s, width), lse.reshape(s, width)


def _merge_kernel(n_groups, *refs):
    o_refs, l_refs, out_ref = refs[:n_groups], refs[n_groups:2 * n_groups], refs[2 * n_groups]
    lses = [r[...] for r in l_refs]
    m = functools.reduce(jnp.maximum, lses)
    ws = [jnp.exp2(l - m) for l in lses]
    tot = functools.reduce(jnp.add, ws)
    acc = functools.reduce(jnp.add, [w * r[...] for w, r in zip(ws, o_refs)])
    out_ref[...] = (acc / tot).astype(out_ref.dtype)


def _merge_groups(outs, lses, *, tm):
    s, width = outs[0].shape
    n = len(outs)
    spec = pl.BlockSpec((tm, width), lambda i: (i, 0))
    return pl.pallas_call(
        functools.partial(_merge_kernel, n),
        out_shape=jax.ShapeDtypeStruct((s, width), jnp.bfloat16),
        grid=(s // tm,),
        in_specs=[spec] * (2 * n),
        out_specs=spec,
        compiler_params=_cparams(("parallel",), 2 * (2 * n * tm * width * 4 + tm * width * 2)),
        name="dilated_merge",
    )(*outs, *lses)


ONES_ROWS = 16


def _transpose_kernel(x_ref, o_ref):
    o_ref[...] = x_ref[...].astype(jnp.float32).T.astype(o_ref.dtype)


def _transpose_cols(x, col0, ncols, *, ts, tc):
    s = x.shape[0]
    return pl.pallas_call(
        _transpose_kernel,
        out_shape=jax.ShapeDtypeStruct((ncols, s), x.dtype),
        grid=(s // ts, ncols // tc),
        in_specs=[pl.BlockSpec((ts, tc), lambda i, c: (i, col0 // tc + c))],
        out_specs=pl.BlockSpec((tc, ts), lambda i, c: (c, i)),
        compiler_params=_cparams(("parallel", "parallel"), 4 * ts * tc * 2 + 3 * ts * tc * 4),
        name="transpose_cols",
    )(x)


def _online_step_t(q, k, vt_aug, m_ref, acc_ref):
    st = _dot_nt(k, q)
    m_prev = m_ref[...]
    m_new = jnp.maximum(m_prev, jnp.max(st, axis=0, keepdims=True))
    pt = jnp.exp2(st - m_new).astype(jnp.bfloat16)
    acc_ref[...] = jnp.exp2(m_prev - m_new) * acc_ref[...] + _dot(vt_aug, pt)
    m_ref[...] = m_new


def _with_ones_rows(vt):
    return jnp.concatenate([vt, jnp.ones((ONES_ROWS, vt.shape[1]), vt.dtype)], axis=0)


def _normalised(acc_ref, dv):
    return acc_ref[:dv, :] / acc_ref[dv:dv + 1, :]


def _diff_kernel(out_scale, lam_ref, q1_ref, q2_ref, k1_ref, k2_ref, vt_ref, g_ref, o_ref, m_ref, acc_ref):
    kb = pl.program_id(2)
    dv = vt_ref.shape[0]

    @pl.when(kb == 0)
    def _():
        m_ref[...] = jnp.full(m_ref.shape, NEG_INF, jnp.float32)
        acc_ref[...] = jnp.zeros(acc_ref.shape, jnp.float32)

    vt_aug = _with_ones_rows(vt_ref[...])
    _online_step_t(q1_ref[...], k1_ref[...], vt_aug, m_ref.at[0], acc_ref.at[0])
    _online_step_t(q2_ref[...], k2_ref[...], vt_aug, m_ref.at[1], acc_ref.at[1])

    @pl.when(kb == pl.num_programs(2) - 1)
    def _():
        o = (_normalised(acc_ref.at[0], dv) - lam_ref[0] * _normalised(acc_ref.at[1], dv)).T
        ms = jnp.mean(o * o, axis=-1, keepdims=True)
        o_ref[...] = (o * lax.rsqrt(ms + RMS_EPS) * g_ref[...] * out_scale).astype(o_ref.dtype)


def _diff_attention(qkv, vt, lam, subln_g, out_scale, *, q_col, k_col, heads, tq, tk):
    s = qkv.shape[0]
    dv = 2 * HEAD_DIM
    q_spec = lambda off: pl.BlockSpec((tq, HEAD_DIM), lambda h, i, j: (i, q_col + 2 * h + off))
    k_spec = lambda off: pl.BlockSpec((tk, HEAD_DIM), lambda h, i, j: (j, k_col + 2 * h + off))
    est = (2 * (2 * tq * HEAD_DIM * 2 + 2 * tk * HEAD_DIM * 2 + tk * dv * 2 + tq * dv * 2)
           + 2 * tq * (dv + 3 * ONES_ROWS) * 4 + 6 * tq * tk * 4)
    return pl.pallas_call(
        functools.partial(_diff_kernel, out_scale),
        out_shape=jax.ShapeDtypeStruct((s, heads * dv), jnp.bfloat16),
        grid=(heads, s // tq, s // tk),
        in_specs=[pl.BlockSpec(memory_space=pltpu.SMEM),
                  q_spec(0), q_spec(1), k_spec(0), k_spec(1),
                  pl.BlockSpec((dv, tk), lambda h, i, j: (h, j)),
                  pl.BlockSpec((1, dv), lambda h, i, j: (0, 0))],
        out_specs=pl.BlockSpec((tq, dv), lambda h, i, j: (i, h)),
        scratch_shapes=[pltpu.VMEM((2, 1, tq), jnp.float32), pltpu.VMEM((2, dv + ONES_ROWS, tq), jnp.float32)],
        compiler_params=_cparams(("parallel", "parallel", "arbitrary"), est),
        name="diff_attn",
    )(lam, qkv, qkv, qkv, qkv, vt, subln_g)


def _gqa_kernel(group, q_ref, k_ref, vt_ref, o_ref, m_ref, acc_ref):
    kb = pl.program_id(2)
    tq = q_ref.shape[0]

    @pl.when(kb == 0)
    def _():
        m_ref[...] = jnp.full(m_ref.shape, NEG_INF, jnp.float32)
        acc_ref[...] = jnp.zeros(acc_ref.shape, jnp.float32)

    q = jnp.concatenate([q_ref[:, g * HEAD_DIM:(g + 1) * HEAD_DIM] for g in range(group)], axis=0)
    _online_step_t(q, k_ref[...], _with_ones_rows(vt_ref[...]), m_ref, acc_ref)

    @pl.when(kb == pl.num_programs(2) - 1)
    def _():
        o = _normalised(acc_ref, HEAD_DIM)
        for g in range(group):
            o_ref[:, g * HEAD_DIM:(g + 1) * HEAD_DIM] = o[:, g * tq:(g + 1) * tq].T.astype(o_ref.dtype)


def _gqa_attention(qkv, vt, *, q_col, k_col, n_kv, group, tq, tk):
    s = qkv.shape[0]
    lanes = group * tq
    est = (2 * (2 * tq * group * HEAD_DIM * 2 + 2 * tk * HEAD_DIM * 2)
           + lanes * (HEAD_DIM + 3 * ONES_ROWS) * 4 + 3 * lanes * tk * 4)
    return pl.pallas_call(
        functools.partial(_gqa_kernel, group),
        out_shape=jax.ShapeDtypeStruct((s, n_kv * group * HEAD_DIM), jnp.bfloat16),
        grid=(n_kv, s // tq, s // tk),
        in_specs=[pl.BlockSpec((tq, group * HEAD_DIM), lambda kh, i, j: (i, q_col // group + kh)),
                  pl.BlockSpec((tk, HEAD_DIM), lambda kh, i, j: (j, k_col + kh)),
                  pl.BlockSpec((HEAD_DIM, tk), lambda kh, i, j: (kh, j))],
        out_specs=pl.BlockSpec((tq, group * HEAD_DIM), lambda kh, i, j: (i, kh)),
        scratch_shapes=[pltpu.VMEM((1, lanes), jnp.float32),
                        pltpu.VMEM((HEAD_DIM + ONES_ROWS, lanes), jnp.float32)],
        compiler_params=_cparams(("parallel", "parallel", "arbitrary"), est),
        name="gqa_attn",
    )(qkv, qkv, vt)


def _ln_inplace(of_ref, ob_ref, g_ref, b_ref, tn):
    d = of_ref.shape[1]
    chunks = [slice(c * tn, (c + 1) * tn) for c in range(d // tn)]
    tot = functools.reduce(jnp.add, [jnp.sum(of_ref[:, sl], axis=-1, keepdims=True) for sl in chunks])
    mu = tot * (1.0 / d)
    sq = functools.reduce(jnp.add, [jnp.sum(jnp.square(of_ref[:, sl] - mu), axis=-1, keepdims=True)
                                    for sl in chunks])
    rstd = lax.rsqrt(sq * (1.0 / d) + LN_EPS)
    for sl in chunks:
        out = (of_ref[:, sl] - mu) * rstd * g_ref[:, sl] + b_ref[:, sl]
        of_ref[:, sl] = out
        ob_ref[:, sl] = out.astype(jnp.bfloat16)


def _outproj_kernel(tn, a_ref, w_ref, x_ref, g_ref, b_ref, of_ref, ob_ref):
    j = pl.program_id(1)
    d = of_ref.shape[1]
    nj = d // tn
    y = _dot(a_ref[...], w_ref[...]) + DEEPNORM_ALPHA * x_ref[...]
    for c in range(nj):
        @pl.when(j == c)
        def _(c=c):
            of_ref[:, c * tn:(c + 1) * tn] = y

    @pl.when(j == nj - 1)
    def _():
        _ln_inplace(of_ref, ob_ref, g_ref, b_ref, tn)


def _outproj_ln(a, w, x, g, b, *, tm, tn):
    s, k = a.shape
    d = w.shape[1]
    est = 2 * (tm * k * 2 + k * tn * 2 + tm * tn * 4 + tm * d * 4 + tm * d * 2) + 4 * tm * tn * 4
    row = pl.BlockSpec((tm, d), lambda i, j: (i, 0))
    vec = pl.BlockSpec((1, d), lambda i, j: (0, 0))
    return pl.pallas_call(
        functools.partial(_outproj_kernel, tn),
        out_shape=[jax.ShapeDtypeStruct((s, d), jnp.float32), jax.ShapeDtypeStruct((s, d), jnp.bfloat16)],
        grid=(s // tm, d // tn),
        in_specs=[pl.BlockSpec((tm, k), lambda i, j: (i, 0)),
                  pl.BlockSpec((k, tn), lambda i, j: (0, j)),
                  pl.BlockSpec((tm, tn), lambda i, j: (i, j)), vec, vec],
        out_specs=[row, row],
        compiler_params=_cparams(("parallel", "arbitrary"), est),
        name="out_proj_ln",
    )(a, w, x, g, b)


def _top_rows(a, k):
    vals = []
    for _ in range(k):
        m = jnp.max(a, axis=0, keepdims=True)
        vals.append(m)
        a = jnp.where(a == m, -jnp.inf, a)
    return vals


def _router_kernel(x_ref, wq_ref, keys_ref, tau_ref, e1_ref, s2_ref, e2_ref):
    half = PEER_QDIM // 2
    q = _dot(x_ref[...], wq_ref[...]).astype(jnp.bfloat16)
    s1 = _dot_nt(keys_ref[0, 0], q[:, :half])
    s2 = _dot_nt(keys_ref[0, 1], q[:, half:])
    k = PEER_TOPK
    v1 = _top_rows(s1, k + 1)
    v2 = _top_rows(s2, k + 1)
    rid = lax.broadcasted_iota(jnp.int32, (k, s1.shape[1]), 0)
    sv2 = functools.reduce(lambda acc, i: jnp.where(rid == i, v2[i], acc), range(1, k),
                           jnp.broadcast_to(v2[0], rid.shape))
    cand = jnp.concatenate([v1[a] + sv2 for a in range(k)], axis=0)
    c = _top_rows(cand, k + 1)
    c_next = jnp.maximum(c[k], jnp.maximum(v1[k] + v2[0], v1[0] + v2[k]))
    thr = 0.5 * (c[k - 1] + c_next)
    ee2 = jnp.exp(sv2 - v2[0])
    z = functools.reduce(jnp.add, [
        jnp.exp(v1[a] - v1[0]) * jnp.sum(jnp.where(sv2 >= thr - v1[a], ee2, 0.0), axis=0, keepdims=True)
        for a in range(k)])
    tau_ref[0] = thr - s1
    e1_ref[0] = jnp.exp(s1 - v1[0]) / z
    s2_ref[0] = s2
    e2_ref[0] = jnp.exp(s2 - v2[0])


def _peer_router(xb, wq, keys, *, tt):
    t, d = xb.shape
    out = jax.ShapeDtypeStruct((PEER_HEADS, PEER_NKEYS, t), jnp.float32)
    ospec = pl.BlockSpec((1, PEER_NKEYS, tt), lambda i, h: (h, 0, i))
    est = 2 * (tt * d * 2 + d * PEER_QDIM * 2 + 4 * PEER_NKEYS * tt * 4) + 24 * PEER_QDIM * tt * 4
    return pl.pallas_call(
        _router_kernel,
        out_shape=[out] * 4,
        grid=(t // tt, PEER_HEADS),
        in_specs=[pl.BlockSpec((tt, d), lambda i, h: (i, 0)),
                  pl.BlockSpec((d, PEER_QDIM), lambda i, h: (0, h)),
                  pl.BlockSpec((1, 2, PEER_NKEYS, PEER_QDIM // 2), lambda i, h: (h, 0, 0, 0))],
        out_specs=[ospec] * 4,
        compiler_params=_cparams(("parallel", "arbitrary"), est),
        name="peer_router",
    )(xb, wq, keys)


def _gelu_tanh(h):
    c = math.sqrt(2.0 / math.pi)
    return 0.5 * h * (1.0 + jnp.tanh(c * (h + 0.044715 * (h * h * h))))


def _peer_dense_kernel(xt_ref, u_ref, vt_ref, tau_ref, e1_ref, s2_ref, e2_ref, o_ref, hid_a, hid_b, p_ref):
    e = pl.program_id(1)
    n_i = u_ref.shape[0] // PEER_NKEYS

    @pl.when(e == 0)
    def _():
        o_ref[...] = jnp.zeros(o_ref.shape, jnp.float32)
        hid_b[...] = jnp.zeros(hid_b.shape, jnp.float32)

    def step(hid_w, hid_r):
        i0 = jnp.maximum(e - 1, 0) * n_i
        for il in range(n_i):
            rows = slice(il * PEER_NKEYS, (il + 1) * PEER_NKEYS)
            gate = None
            for h in range(PEER_HEADS):
                tau = tau_ref[h, pl.ds(i0 + il, 1), :]
                e1 = e1_ref[h, pl.ds(i0 + il, 1), :]
                w = jnp.where(s2_ref[h] >= tau, e2_ref[h], 0.0) * e1
                gate = w if gate is None else gate + w
            p_ref[rows, :] = (gate * _gelu_tanh(hid_r[rows, :])).astype(jnp.bfloat16)
        hid_w[...] = _dot(u_ref[...], xt_ref[...])
        o_ref[...] += _dot(vt_ref[...], p_ref[...])

    pl.when(e % 2 == 0)(functools.partial(step, hid_a, hid_b))
    pl.when(e % 2 == 1)(functools.partial(step, hid_b, hid_a))


def _peer_dense(xtb, ub, vtb, tau, e1, s2, e2, *, tt, te):
    d, t = xtb.shape
    n_e = ub.shape[0] // te
    once = pl.Buffered(1)
    rspec = pl.BlockSpec((PEER_HEADS, PEER_NKEYS, tt), lambda i, e: (0, 0, i), pipeline_mode=once)
    est = (d * tt * 2 + 2 * te * d * 2 + 2 * d * te * 2 + 4 * PEER_HEADS * PEER_NKEYS * tt * 4
           + 2 * d * tt * 4 + te * tt * 2 + 2 * te * tt * 4 + 4 * te * tt * 4)
    return pl.pallas_call(
        _peer_dense_kernel,
        out_shape=jax.ShapeDtypeStruct((d, t), jnp.float32),
        grid=(t // tt, n_e + 1),
        in_specs=[pl.BlockSpec((d, tt), lambda i, e: (0, i), pipeline_mode=once),
                  pl.BlockSpec((te, d), lambda i, e: (jnp.minimum(e, n_e - 1), 0)),
                  pl.BlockSpec((d, te), lambda i, e: (0, jnp.maximum(e - 1, 0))),
                  rspec, rspec, rspec, rspec],
        out_specs=pl.BlockSpec((d, tt), lambda i, e: (0, i)),
        scratch_shapes=[pltpu.VMEM((te, tt), jnp.float32), pltpu.VMEM((te, tt), jnp.float32),
                        pltpu.VMEM((te, tt), jnp.bfloat16)],
        compiler_params=_cparams(("parallel", "arbitrary"), est),
        name="peer_dense",
    )(xtb, ub, vtb, tau, e1, s2, e2)


def _ffn_ln_kernel(tn, x_ref, ft_ref, g_ref, b_ref, of_ref, ob_ref):
    for c in range(of_ref.shape[1] // tn):
        sl = slice(c * tn, (c + 1) * tn)
        of_ref[:, sl] = DEEPNORM_ALPHA * x_ref[:, sl] + ft_ref[sl, :].T
    _ln_inplace(of_ref, ob_ref, g_ref, b_ref, tn)


def _ffn_ln(x, ffn_t, g, b, *, tm):
    t, d = x.shape
    row = pl.BlockSpec((tm, d), lambda i: (i, 0))
    vec = pl.BlockSpec((1, d), lambda i: (0, 0))
    return pl.pallas_call(
        functools.partial(_ffn_ln_kernel, 512),
        out_shape=[jax.ShapeDtypeStruct((t, d), jnp.float32), jax.ShapeDtypeStruct((t, d), jnp.bfloat16)],
        grid=(t // tm,),
        in_specs=[row, pl.BlockSpec((d, tm), lambda i: (0, i)), vec, vec],
        out_specs=[row, row],
        compiler_params=_cparams(("parallel",), 2 * (3 * tm * d * 4 + tm * d * 2) + 3 * tm * d * 4),
        name="ffn_ln",
    )(x, ffn_t, g, b)


def _col_tiles(sizes, modes, tn):
    out, lo = [], 0
    for size, mode in zip(sizes, modes):
        assert size % tn == 0
        out.append((lo // tn, (lo + size) // tn, mode))
        lo += size
    return out


def _even_mixer(xb, w_in, sink, tabs):
    q_scale = QK_SCALE * LOG2E
    rope_q, rope_k, plain = (0, ROPE_DIMS // 2, q_scale, None), (0, ROPE_DIMS // 2, 1.0, None), (None, 0, 1.0, None)
    modes = _col_tiles(AB_SIZES, [rope_q, rope_k, plain, rope_q, rope_k, plain], 512)
    gains = jnp.ones((8, HEAD_DIM), jnp.float32)
    qkv = _project(xb, w_in, tabs, gains, modes, tm=1024 if xb.shape[0] % 1024 == 0 else xb.shape[0], tn=512)
    cols = [0]
    for sz in AB_SIZES:
        cols.append(cols[-1] + sz // HEAD_DIM)
    oa = _window_attention(qkv, sink.reshape(-1), q_col=cols[0], k_col=cols[1], v_col=cols[2],
                           n_kv=A_KV_HEADS, group=A_GROUP, blk=A_WINDOW, window=A_WINDOW)
    gw = B_GROUP_HEADS * HEAD_DIM
    outs, lses = [], []
    for g, (win, dil) in enumerate(B_PAIRS):
        take = lambda c: qkv[:, (c * HEAD_DIM + g * gw):(c * HEAD_DIM + (g + 1) * gw)]
        o, l = _dilated_attention(take(cols[3]), take(cols[4]), take(cols[5]), dil=dil, blk=win // (2 * dil))
        outs.append(o)
        lses.append(l)
    ob = _merge_groups(outs, lses, tm=512)
    return jnp.concatenate([oa, ob], axis=-1)


def _odd_mixer(xb, w_in, lam, subln_g, q_norm_g, k_norm_g, tabs, lambda_init):
    q_scale = QK_SCALE * LOG2E
    half_ax = HEAD_DIM // 4
    rope_q, rope_k, plain = (0, ROPE_DIMS // 2, q_scale, None), (0, ROPE_DIMS // 2, 1.0, None), (None, 0, 1.0, None)
    ax_q, ax_k = (1, half_ax, q_scale, 0), (1, half_ax, 1.0, 1)
    modes = _col_tiles(CD_SIZES, [rope_q, rope_k, plain, ax_q, ax_k, plain], 512)
    gains = jnp.zeros((8, HEAD_DIM), jnp.float32).at[0].set(q_norm_g).at[1].set(k_norm_g)
    qkv = _project(xb, w_in, tabs, gains, modes, tm=1024 if xb.shape[0] % 1024 == 0 else xb.shape[0], tn=512)
    cols = [0]
    for sz in CD_SIZES:
        cols.append(cols[-1] + sz // HEAD_DIM)
    s = xb.shape[0]
    ts = min(512, s)
    vct = _transpose_cols(qkv, cols[2] * HEAD_DIM, CD_SIZES[2], ts=ts, tc=512)
    vdt = _transpose_cols(qkv, cols[5] * HEAD_DIM, CD_SIZES[5], ts=ts, tc=512)
    oc = _diff_attention(qkv, vct, lam, subln_g.reshape(1, -1), 1.0 - lambda_init, q_col=cols[0], k_col=cols[1],
                         heads=C_HEADS, tq=min(1024, s), tk=min(2048, s))
    od = _gqa_attention(qkv, vdt, q_col=cols[3], k_col=cols[4], n_kv=D_KV_HEADS, group=D_GROUP,
                        tq=min(256, s), tk=min(4096, s))
    return jnp.concatenate([oc, od], axis=-1)


def _peer_ffn(xf, xb, wq, keys, u, v, g, b):
    t = xb.shape[0]
    tt = min(512, t)
    tau, e1, s2, e2 = _peer_router(xb, wq.astype(jnp.bfloat16), keys.astype(jnp.bfloat16), tt=tt)
    ffn_t = _peer_dense(xb.T, u.astype(jnp.bfloat16), v.T.astype(jnp.bfloat16), tau, e1, s2, e2, tt=tt, te=512)
    return _ffn_ln(xf, ffn_t, g.reshape(1, -1), b.reshape(1, -1), tm=min(256, t))


def kernel(x, w_in_ab, sink_a, w_out_ab, w_in_cd, lam_q1, lam_k1, lam_q2, lam_k2, subln_g, q_norm_g, k_norm_g,
           w_out_cd, ln_mix_g, ln_mix_b, peer_wq, peer_keys, peer_u, peer_v, ln_ffn_g, ln_ffn_b):
    b, s, d = x.shape
    assert b == 1 and d == D_MODEL
    bf = jnp.bfloat16
    pos = jnp.arange(s)
    rows, cols = pos // GRID_W, pos % GRID_W
    tab_partial = _rope_tables([(0, ROPE_DIMS, pos, ROPE_THETA)])
    tab_axial = _rope_tables([(0, HEAD_DIM // 2, rows, AXIAL_THETA), (HEAD_DIM // 2, HEAD_DIM // 2, cols, AXIAL_THETA)])
    xf = x[0]
    xb = xf.astype(bf)
    for layer in range(DEPTH):
        i = layer // 2
        if layer % 2 == 0:
            mix_in = _even_mixer(xb, w_in_ab[i].astype(bf), sink_a[i], tab_partial)
            w_out = w_out_ab[i]
        else:
            lambda_init = 0.8 - 0.6 * math.exp(-0.3 * layer)
            f32 = jnp.float32
            lam = (jnp.exp(jnp.sum(lam_q1[i].astype(f32) * lam_k1[i].astype(f32)))
                   - jnp.exp(jnp.sum(lam_q2[i].astype(f32) * lam_k2[i].astype(f32))) + lambda_init)
            mix_in = _odd_mixer(xb, w_in_cd[i].astype(bf), lam.reshape(1), subln_g[i], q_norm_g[i], k_norm_g[i],
                                jnp.concatenate([tab_partial, tab_axial]), lambda_init)
            w_out = w_out_cd[i]
        xf, xb = _outproj_ln(mix_in, w_out.astype(bf), xf, ln_mix_g[layer].reshape(1, -1),
                             ln_mix_b[layer].reshape(1, -1), tm=min(512, s), tn=512)
        xf, xb = _peer_ffn(xf, xb, peer_wq[layer], peer_keys[layer], peer_u[layer], peer_v[layer],
                           ln_ffn_g[layer], ln_ffn_b[layer])
    return xf[None]
```

```python
import functools
import math

import jax
import jax.numpy as jnp
from jax import lax
from jax.experimental import pallas as pl
from jax.experimental.pallas import tpu as pltpu

D_MODEL = 4096
DEPTH = 2
HEAD_DIM = 128
N_HEADS = D_MODEL // HEAD_DIM
ROPE_DIMS = HEAD_DIM // 4
ROPE_THETA = 500000.0
AXIAL_THETA = 10000.0
GRID_W = 64
A_HEADS = N_HEADS // 2
A_KV_HEADS = A_HEADS // 4
A_GROUP = A_HEADS // A_KV_HEADS
A_WINDOW = 128
B_PAIRS = ((128, 1), (512, 4), (2048, 16))
B_GROUP_HEADS = N_HEADS // 4
B_HEADS = len(B_PAIRS) * B_GROUP_HEADS
C_HEADS = N_HEADS // 4
D_HEADS = N_HEADS // 2
D_KV_HEADS = D_HEADS // 4
D_GROUP = D_HEADS // D_KV_HEADS
PEER_HEADS = 8
PEER_NKEYS = 128
PEER_EXPERTS = PEER_NKEYS * PEER_NKEYS
PEER_QDIM = 256
PEER_TOPK = 16
DEEPNORM_ALPHA = (2 * DEPTH) ** 0.25
LN_EPS = 1e-5
RMS_EPS = 1e-6
NEG_INF = -1e30
LOG2E = 1.4426950408889634
QK_SCALE = HEAD_DIM ** -0.5

AB_SIZES = (A_HEADS * HEAD_DIM, A_KV_HEADS * HEAD_DIM, A_KV_HEADS * HEAD_DIM,
            B_HEADS * HEAD_DIM, B_HEADS * HEAD_DIM, B_HEADS * HEAD_DIM)
CD_SIZES = (C_HEADS * 2 * HEAD_DIM, C_HEADS * 2 * HEAD_DIM, C_HEADS * 2 * HEAD_DIM,
            D_HEADS * HEAD_DIM, D_KV_HEADS * HEAD_DIM, D_KV_HEADS * HEAD_DIM)

LANES = 128
V7X_VMEM_BYTES = 64 * 1024 * 1024


def _vmem_limit(nbytes):
    return int(min(max(nbytes * 5 // 4 + (4 << 20), 16 << 20), V7X_VMEM_BYTES - (6 << 20)))


def _cparams(sem, nbytes):
    return pltpu.CompilerParams(dimension_semantics=sem, vmem_limit_bytes=_vmem_limit(nbytes))


def _dot(a, b):
    return jnp.dot(a, b, preferred_element_type=jnp.float32)


def _dot_nt(a, b):
    return lax.dot_general(a, b, (((1,), (1,)), ((), ())), preferred_element_type=jnp.float32)


def _rope_tables(spans):
    s = spans[0][2].shape[0]
    lane = jnp.arange(HEAD_DIM)
    ang = jnp.zeros((s, HEAD_DIM), jnp.float32)
    lo = jnp.zeros((HEAD_DIM,), bool)
    hi = jnp.zeros((HEAD_DIM,), bool)
    for start, n, pos, theta in spans:
        half = n // 2
        inv = theta ** (-jnp.arange(half, dtype=jnp.float32) / half)
        in_span = (lane >= start) & (lane < start + n)
        inv_lane = inv[(lane - start) % half]
        ang = jnp.where(in_span[None, :], pos.astype(jnp.float32)[:, None] * inv_lane[None, :], ang)
        lo = lo | (in_span & (lane < start + half))
        hi = hi | (in_span & (lane >= start + half))
    cos, sin = jnp.cos(ang), jnp.sin(ang)
    c = jnp.where((lo | hi)[None, :], cos, 1.0)
    sa = jnp.where(lo[None, :], -sin, 0.0)
    sb = jnp.where(hi[None, :], sin, 0.0)
    return jnp.stack([c, sa, sb])


def _proj_kernel(col_modes, tn, a_ref, b_ref, tabs_ref, gains_ref, o_ref):
    j = pl.program_id(1)
    acc = _dot(a_ref[...], b_ref[...])

    def epilogue(mode):
        tab, half, scale, gain = mode
        for c in range(tn // HEAD_DIM):
            sl = slice(c * HEAD_DIM, (c + 1) * HEAD_DIM)
            xc = acc[:, sl]
            if gain is not None:
                ms = jnp.mean(xc * xc, axis=-1, keepdims=True)
                xc = xc * lax.rsqrt(ms + RMS_EPS) * gains_ref[gain:gain + 1, :]
            if tab is not None:
                xc = (xc * tabs_ref[3 * tab] + pltpu.roll(xc, HEAD_DIM - half, 1) * tabs_ref[3 * tab + 1]
                      + pltpu.roll(xc, half, 1) * tabs_ref[3 * tab + 2])
            if scale != 1.0:
                xc = xc * scale
            o_ref[:, sl] = xc.astype(o_ref.dtype)

    for lo, hi, mode in col_modes:
        pl.when((j >= lo) & (j < hi))(functools.partial(epilogue, mode))


def _project(a, b, tabs, gains, col_modes, *, tm, tn):
    s, k = a.shape
    n = b.shape[1]
    nt = tabs.shape[0]
    est = 2 * (tm * k * 2 + k * tn * 2 + tm * tn * 2 + nt * tm * HEAD_DIM * 4) + 3 * tm * tn * 4
    return pl.pallas_call(
        functools.partial(_proj_kernel, col_modes, tn),
        out_shape=jax.ShapeDtypeStruct((s, n), jnp.bfloat16),
        grid=(s // tm, n // tn),
        in_specs=[pl.BlockSpec((tm, k), lambda i, j: (i, 0)),
                  pl.BlockSpec((k, tn), lambda i, j: (0, j)),
                  pl.BlockSpec((nt, tm, HEAD_DIM), lambda i, j: (0, i, 0)),
                  pl.BlockSpec(gains.shape, lambda i, j: (0, 0))],
        out_specs=pl.BlockSpec((tm, tn), lambda i, j: (i, j)),
        compiler_params=_cparams(("parallel", "arbitrary"), est),
        name="in_proj",
    )(a, b, tabs, gains)


def _band_mask(n, blk, window, seq_len):
    rows = lax.broadcasted_iota(jnp.int32, (blk, 3 * blk), 0)
    cols = lax.broadcasted_iota(jnp.int32, (blk, 3 * blk), 1)
    kpos = (n - 1) * blk + cols
    dist = jnp.abs(rows + blk - cols)
    return (dist <= window) & (kpos >= 0) & (kpos < seq_len)


def _window_kernel(blk, window, seq_len, group, sink_ref, q_ref, kp_ref, kc_ref, kn_ref,
                   vp_ref, vc_ref, vn_ref, o_ref):
    kh = pl.program_id(0)
    n = pl.program_id(1)
    mask = _band_mask(n, blk, window, seq_len)
    kcat = jnp.concatenate([kp_ref[...], kc_ref[...], kn_ref[...]], axis=0)
    vcat = jnp.concatenate([vp_ref[...], vc_ref[...], vn_ref[...]], axis=0)
    for g in range(group):
        sl = slice(g * HEAD_DIM, (g + 1) * HEAD_DIM)
        s = jnp.where(mask, _dot_nt(q_ref[:, sl], kcat), NEG_INF)
        sink = sink_ref[kh * group + g] * LOG2E
        m = jnp.maximum(jnp.max(s, axis=-1, keepdims=True), sink)
        p = jnp.exp2(s - m)
        denom = jnp.sum(p, axis=-1, keepdims=True) + jnp.exp2(sink - m)
        o = _dot(p.astype(jnp.bfloat16), vcat) / denom
        o_ref[:, sl] = o.astype(o_ref.dtype)


def _window_attention(qkv, sink, *, q_col, k_col, v_col, n_kv, group, blk, window):
    s = qkv.shape[0]
    nb = s // blk
    prev = lambda n: jnp.maximum(n - 1, 0)
    nxt = lambda n: jnp.minimum(n + 1, nb - 1)
    kv_spec = lambda col, f: pl.BlockSpec((blk, HEAD_DIM), lambda kh, n: (f(n), col + kh))
    ident = lambda n: n
    est = 2 * (2 * blk * group * HEAD_DIM * 2 + 6 * blk * HEAD_DIM * 2) + 8 * blk * 3 * blk * 4
    return pl.pallas_call(
        functools.partial(_window_kernel, blk, window, s, group),
        out_shape=jax.ShapeDtypeStruct((s, n_kv * group * HEAD_DIM), jnp.bfloat16),
        grid=(n_kv, nb),
        in_specs=[pl.BlockSpec(memory_space=pltpu.SMEM),
                  pl.BlockSpec((blk, group * HEAD_DIM), lambda kh, n: (n, q_col // group + kh)),
                  kv_spec(k_col, prev), kv_spec(k_col, ident), kv_spec(k_col, nxt),
                  kv_spec(v_col, prev), kv_spec(v_col, ident), kv_spec(v_col, nxt)],
        out_specs=pl.BlockSpec((blk, group * HEAD_DIM), lambda kh, n: (n, kh)),
        compiler_params=_cparams(("parallel", "arbitrary"), est),
        name="window_attn",
    )(sink, qkv, qkv, qkv, qkv, qkv, qkv, qkv)


def _dilated_kernel(blk, seq_len, heads, q_ref, kp_ref, kc_ref, kn_ref, vp_ref, vc_ref, vn_ref,
                    o_ref, lse_ref):
    n = pl.program_id(0)
    mask = _band_mask(n, blk, blk, seq_len)
    for h in range(heads):
        sl = slice(h * HEAD_DIM, (h + 1) * HEAD_DIM)
        kcat = jnp.concatenate([kp_ref[:, sl], kc_ref[:, sl], kn_ref[:, sl]], axis=0)
        vcat = jnp.concatenate([vp_ref[:, sl], vc_ref[:, sl], vn_ref[:, sl]], axis=0)
        s = jnp.where(mask, _dot_nt(q_ref[:, sl], kcat), NEG_INF)
        m = jnp.max(s, axis=-1, keepdims=True)
        p = jnp.exp2(s - m)
        denom = jnp.sum(p, axis=-1, keepdims=True)
        o_ref[:, sl] = _dot(p.astype(jnp.bfloat16), vcat) / denom
        lse_ref[:, sl] = jnp.broadcast_to(m + jnp.log2(denom), (blk, HEAD_DIM))


def _dilated_attention(q, k, v, *, dil, blk):
    s, width = q.shape
    heads = width // HEAD_DIM
    fold_len = s // dil
    nb = fold_len // blk
    fold = lambda t: t.reshape(fold_len, dil * width)
    spec = lambda f: pl.BlockSpec((blk, width), lambda n, r: (f(n), r))
    prev = lambda n: jnp.maximum(n - 1, 0)
    nxt = lambda n: jnp.minimum(n + 1, nb - 1)
    ident = lambda n: n
    est = 2 * (7 * blk * width * 2 + 2 * blk * width * 4) + 8 * blk * 3 * blk * 4
    qf, kf, vf = fold(q), fold(k), fold(v)
    o, lse = pl.pallas_call(
        functools.partial(_dilated_kernel, blk, fold_len, heads),
        out_shape=[jax.ShapeDtypeStruct((fold_len, dil * width), jnp.float32)] * 2,
        grid=(nb, dil),
        in_specs=[spec(ident), spec(prev), spec(ident), spec(nxt), spec(prev), spec(ident), spec(nxt)],
        out_specs=[spec(ident), spec(ident)],
        compiler_params=_cparams(("parallel", "arbitrary"), est),
        name=f"dilated_attn_{dil}",
    )(qf, kf, kf, kf, vf, vf, vf)
    return o.reshape(s, width), lse.reshape(s, width)


def _merge_kernel(n_groups, *refs):
    o_refs, l_refs, out_ref = refs[:n_groups], refs[n_groups:2 * n_groups], refs[2 * n_groups]
    lses = [r[...] for r in l_refs]
    m = functools.reduce(jnp.maximum, lses)
    ws = [jnp.exp2(l - m) for l in lses]
    tot = functools.reduce(jnp.add, ws)
    acc = functools.reduce(jnp.add, [w * r[...] for w, r in zip(ws, o_refs)])
    out_ref[...] = (acc / tot).astype(out_ref.dtype)


def _merge_groups(outs, lses, *, tm):
    s, width = outs[0].shape
    n = len(outs)
    spec = pl.BlockSpec((tm, width), lambda i: (i, 0))
    return pl.pallas_call(
        functools.partial(_merge_kernel, n),
        out_shape=jax.ShapeDtypeStruct((s, width), jnp.bfloat16),
        grid=(s // tm,),
        in_specs=[spec] * (2 * n),
        out_specs=spec,
        compiler_params=_cparams(("parallel",), 2 * (2 * n * tm * width * 4 + tm * width * 2)),
        name="dilated_merge",
    )(*outs, *lses)


ONES_ROWS = 16


def _transpose_kernel(x_ref, o_ref):
    o_ref[...] = x_ref[...].astype(jnp.float32).T.astype(o_ref.dtype)


def _transpose_cols(x, col0, ncols, *, ts, tc, out_dtype=None):
    s = x.shape[0]
    out_dtype = out_dtype or x.dtype
    return pl.pallas_call(
        _transpose_kernel,
        out_shape=jax.ShapeDtypeStruct((ncols, s), out_dtype),
        grid=(s // ts, ncols // tc),
        in_specs=[pl.BlockSpec((ts, tc), lambda i, c: (i, col0 // tc + c))],
        out_specs=pl.BlockSpec((tc, ts), lambda i, c: (c, i)),
        compiler_params=_cparams(("parallel", "parallel"), 4 * ts * tc * 4 + 3 * ts * tc * 4),
        name="transpose_cols",
    )(x)


def _cast_kernel(x_ref, o_ref):
    o_ref[...] = x_ref[...].astype(o_ref.dtype)


def _cast_rows(x, out_dtype, *, ts):
    s, d = x.shape
    spec = pl.BlockSpec((ts, d), lambda i: (i, 0))
    return pl.pallas_call(
        _cast_kernel,
        out_shape=jax.ShapeDtypeStruct((s, d), out_dtype),
        grid=(s // ts,),
        in_specs=[spec],
        out_specs=spec,
        compiler_params=_cparams(("parallel",), 2 * ts * d * 6),
        name="cast_rows",
    )(x)


def _online_step_t(q, k, vt_aug, m_ref, acc_ref):
    st = _dot_nt(k, q)
    m_prev = m_ref[...]
    m_new = jnp.maximum(m_prev, jnp.max(st, axis=0, keepdims=True))
    pt = jnp.exp2(st - m_new).astype(jnp.bfloat16)
    acc_ref[...] = jnp.exp2(m_prev - m_new) * acc_ref[...] + _dot(vt_aug, pt)
    m_ref[...] = m_new


def _with_ones_rows(vt):
    return jnp.concatenate([vt, jnp.ones((ONES_ROWS, vt.shape[1]), vt.dtype)], axis=0)


def _normalised(acc_ref, dv):
    return acc_ref[:dv, :] / acc_ref[dv:dv + 1, :]


def _diff_kernel(out_scale, lam_ref, q1_ref, q2_ref, k1_ref, k2_ref, vt_ref, g_ref, o_ref, m_ref, acc_ref):
    kb = pl.program_id(2)
    dv = vt_ref.shape[0]

    @pl.when(kb == 0)
    def _():
        m_ref[...] = jnp.full(m_ref.shape, NEG_INF, jnp.float32)
        acc_ref[...] = jnp.zeros(acc_ref.shape, jnp.float32)

    vt_aug = _with_ones_rows(vt_ref[...])
    _online_step_t(q1_ref[...], k1_ref[...], vt_aug, m_ref.at[0], acc_ref.at[0])
    _online_step_t(q2_ref[...], k2_ref[...], vt_aug, m_ref.at[1], acc_ref.at[1])

    @pl.when(kb == pl.num_programs(2) - 1)
    def _():
        o = (_normalised(acc_ref.at[0], dv) - lam_ref[0] * _normalised(acc_ref.at[1], dv)).T
        ms = jnp.mean(o * o, axis=-1, keepdims=True)
        o_ref[...] = (o * lax.rsqrt(ms + RMS_EPS) * g_ref[...] * out_scale).astype(o_ref.dtype)


def _diff_attention(qkv, vt, lam, subln_g, out_scale, *, q_col, k_col, heads, tq, tk):
    s = qkv.shape[0]
    dv = 2 * HEAD_DIM
    q_spec = lambda off: pl.BlockSpec((tq, HEAD_DIM), lambda h, i, j: (i, q_col + 2 * h + off))
    k_spec = lambda off: pl.BlockSpec((tk, HEAD_DIM), lambda h, i, j: (j, k_col + 2 * h + off))
    est = (2 * (2 * tq * HEAD_DIM * 2 + 2 * tk * HEAD_DIM * 2 + tk * dv * 2 + tq * dv * 2)
           + 2 * tq * (dv + 3 * ONES_ROWS) * 4 + 6 * tq * tk * 4)
    return pl.pallas_call(
        functools.partial(_diff_kernel, out_scale),
        out_shape=jax.ShapeDtypeStruct((s, heads * dv), jnp.bfloat16),
        grid=(heads, s // tq, s // tk),
        in_specs=[pl.BlockSpec(memory_space=pltpu.SMEM),
                  q_spec(0), q_spec(1), k_spec(0), k_spec(1),
                  pl.BlockSpec((dv, tk), lambda h, i, j: (h, j)),
                  pl.BlockSpec((1, dv), lambda h, i, j: (0, 0))],
        out_specs=pl.BlockSpec((tq, dv), lambda h, i, j: (i, h)),
        scratch_shapes=[pltpu.VMEM((2, 1, tq), jnp.float32), pltpu.VMEM((2, dv + ONES_ROWS, tq), jnp.float32)],
        compiler_params=_cparams(("parallel", "parallel", "arbitrary"), est),
        name="diff_attn",
    )(lam, qkv, qkv, qkv, qkv, vt, subln_g)


def _gqa_kernel(group, q_ref, k_ref, vt_ref, o_ref, m_ref, acc_ref):
    kb = pl.program_id(2)
    tq = q_ref.shape[0]

    @pl.when(kb == 0)
    def _():
        m_ref[...] = jnp.full(m_ref.shape, NEG_INF, jnp.float32)
        acc_ref[...] = jnp.zeros(acc_ref.shape, jnp.float32)

    q = jnp.concatenate([q_ref[:, g * HEAD_DIM:(g + 1) * HEAD_DIM] for g in range(group)], axis=0)
    _online_step_t(q, k_ref[...], _with_ones_rows(vt_ref[...]), m_ref, acc_ref)

    @pl.when(kb == pl.num_programs(2) - 1)
    def _():
        o = _normalised(acc_ref, HEAD_DIM)
        for g in range(group):
            o_ref[:, g * HEAD_DIM:(g + 1) * HEAD_DIM] = o[:, g * tq:(g + 1) * tq].T.astype(o_ref.dtype)


def _gqa_attention(qkv, vt, *, q_col, k_col, n_kv, group, tq, tk):
    s = qkv.shape[0]
    lanes = group * tq
    est = (2 * (2 * tq * group * HEAD_DIM * 2 + 2 * tk * HEAD_DIM * 2)
           + lanes * (HEAD_DIM + 3 * ONES_ROWS) * 4 + 3 * lanes * tk * 4)
    return pl.pallas_call(
        functools.partial(_gqa_kernel, group),
        out_shape=jax.ShapeDtypeStruct((s, n_kv * group * HEAD_DIM), jnp.bfloat16),
        grid=(n_kv, s // tq, s // tk),
        in_specs=[pl.BlockSpec((tq, group * HEAD_DIM), lambda kh, i, j: (i, q_col // group + kh)),
                  pl.BlockSpec((tk, HEAD_DIM), lambda kh, i, j: (j, k_col + kh)),
                  pl.BlockSpec((HEAD_DIM, tk), lambda kh, i, j: (kh, j))],
        out_specs=pl.BlockSpec((tq, group * HEAD_DIM), lambda kh, i, j: (i, kh)),
        scratch_shapes=[pltpu.VMEM((1, lanes), jnp.float32),
                        pltpu.VMEM((HEAD_DIM + ONES_ROWS, lanes), jnp.float32)],
        compiler_params=_cparams(("parallel", "parallel", "arbitrary"), est),
        name="gqa_attn",
    )(qkv, qkv, vt)


def _ln_inplace(of_ref, ob_ref, g_ref, b_ref, tn):
    d = of_ref.shape[1]
    chunks = [slice(c * tn, (c + 1) * tn) for c in range(d // tn)]
    tot = functools.reduce(jnp.add, [jnp.sum(of_ref[:, sl], axis=-1, keepdims=True) for sl in chunks])
    mu = tot * (1.0 / d)
    sq = functools.reduce(jnp.add, [jnp.sum(jnp.square(of_ref[:, sl] - mu), axis=-1, keepdims=True)
                                    for sl in chunks])
    rstd = lax.rsqrt(sq * (1.0 / d) + LN_EPS)
    for sl in chunks:
        out = (of_ref[:, sl] - mu) * rstd * g_ref[:, sl] + b_ref[:, sl]
        of_ref[:, sl] = out
        ob_ref[:, sl] = out.astype(jnp.bfloat16)


def _outproj_kernel(tn, a_ref, w_ref, x_ref, g_ref, b_ref, of_ref, ob_ref):
    j = pl.program_id(1)
    d = of_ref.shape[1]
    nj = d // tn
    y = _dot(a_ref[...], w_ref[...]) + DEEPNORM_ALPHA * x_ref[...]
    for c in range(nj):
        @pl.when(j == c)
        def _(c=c):
            of_ref[:, c * tn:(c + 1) * tn] = y

    @pl.when(j == nj - 1)
    def _():
        _ln_inplace(of_ref, ob_ref, g_ref, b_ref, tn)


def _outproj_ln(a, w, x, g, b, *, tm, tn):
    s, k = a.shape
    d = w.shape[1]
    est = 2 * (tm * k * 2 + k * tn * 2 + tm * tn * 4 + tm * d * 4 + tm * d * 2) + 4 * tm * tn * 4
    row = pl.BlockSpec((tm, d), lambda i, j: (i, 0))
    vec = pl.BlockSpec((1, d), lambda i, j: (0, 0))
    return pl.pallas_call(
        functools.partial(_outproj_kernel, tn),
        out_shape=[jax.ShapeDtypeStruct((s, d), jnp.float32), jax.ShapeDtypeStruct((s, d), jnp.bfloat16)],
        grid=(s // tm, d // tn),
        in_specs=[pl.BlockSpec((tm, k), lambda i, j: (i, 0)),
                  pl.BlockSpec((k, tn), lambda i, j: (0, j)),
                  pl.BlockSpec((tm, tn), lambda i, j: (i, j)), vec, vec],
        out_specs=[row, row],
        compiler_params=_cparams(("parallel", "arbitrary"), est),
        name="out_proj_ln",
    )(a, w, x, g, b)


def _top_rows(a, k):
    vals = []
    for _ in range(k):
        m = jnp.max(a, axis=0, keepdims=True)
        vals.append(m)
        a = jnp.where(a == m, -jnp.inf, a)
    return vals


def _router_kernel(x_ref, wq_ref, keys_ref, tau_ref, e1_ref, s2_ref, e2_ref):
    half = PEER_QDIM // 2
    q = _dot(x_ref[...], wq_ref[...]).astype(jnp.bfloat16)
    s1 = _dot_nt(keys_ref[0, 0], q[:, :half])
    s2 = _dot_nt(keys_ref[0, 1], q[:, half:])
    k = PEER_TOPK
    v1 = _top_rows(s1, k + 1)
    v2 = _top_rows(s2, k + 1)
    rid = lax.broadcasted_iota(jnp.int32, (k, s1.shape[1]), 0)
    sv2 = functools.reduce(lambda acc, i: jnp.where(rid == i, v2[i], acc), range(1, k),
                           jnp.broadcast_to(v2[0], rid.shape))
    cand = jnp.concatenate([v1[a] + sv2 for a in range(k)], axis=0)
    c = _top_rows(cand, k + 1)
    c_next = jnp.maximum(c[k], jnp.maximum(v1[k] + v2[0], v1[0] + v2[k]))
    thr = 0.5 * (c[k - 1] + c_next)
    ee2 = jnp.exp(sv2 - v2[0])
    z = functools.reduce(jnp.add, [
        jnp.exp(v1[a] - v1[0]) * jnp.sum(jnp.where(sv2 >= thr - v1[a], ee2, 0.0), axis=0, keepdims=True)
        for a in range(k)])
    tau_ref[0] = thr - s1
    e1_ref[0] = jnp.exp(s1 - v1[0]) / z
    s2_ref[0] = s2
    e2_ref[0] = jnp.exp(s2 - v2[0])


def _peer_router(xb, wq, keys, *, tt):
    t, d = xb.shape
    out = jax.ShapeDtypeStruct((PEER_HEADS, PEER_NKEYS, t), jnp.float32)
    ospec = pl.BlockSpec((1, PEER_NKEYS, tt), lambda i, h: (h, 0, i))
    est = 2 * (tt * d * 2 + d * PEER_QDIM * 2 + 4 * PEER_NKEYS * tt * 4) + 24 * PEER_QDIM * tt * 4
    return pl.pallas_call(
        _router_kernel,
        out_shape=[out] * 4,
        grid=(t // tt, PEER_HEADS),
        in_specs=[pl.BlockSpec((tt, d), lambda i, h: (i, 0)),
                  pl.BlockSpec((d, PEER_QDIM), lambda i, h: (0, h)),
                  pl.BlockSpec((1, 2, PEER_NKEYS, PEER_QDIM // 2), lambda i, h: (h, 0, 0, 0))],
        out_specs=[ospec] * 4,
        compiler_params=_cparams(("parallel", "arbitrary"), est),
        name="peer_router",
    )(xb, wq, keys)


def _gelu_tanh(h):
    c = math.sqrt(2.0 / math.pi)
    return 0.5 * h * (1.0 + jnp.tanh(c * (h + 0.044715 * (h * h * h))))


def _peer_dense_kernel(xt_ref, u_ref, vt_ref, tau_ref, e1_ref, s2_ref, e2_ref, o_ref, hid_a, hid_b, p_ref):
    e = pl.program_id(1)
    n_i = u_ref.shape[0] // PEER_NKEYS

    @pl.when(e == 0)
    def _():
        o_ref[...] = jnp.zeros(o_ref.shape, jnp.float32)
        hid_b[...] = jnp.zeros(hid_b.shape, jnp.float32)

    def step(hid_w, hid_r):
        i0 = jnp.maximum(e - 1, 0) * n_i
        for il in range(n_i):
            rows = slice(il * PEER_NKEYS, (il + 1) * PEER_NKEYS)
            gate = None
            for h in range(PEER_HEADS):
                tau = tau_ref[h, pl.ds(i0 + il, 1), :]
                e1 = e1_ref[h, pl.ds(i0 + il, 1), :]
                w = jnp.where(s2_ref[h] >= tau, e2_ref[h], 0.0) * e1
                gate = w if gate is None else gate + w
            p_ref[rows, :] = (gate * _gelu_tanh(hid_r[rows, :])).astype(jnp.bfloat16)
        hid_w[...] = _dot(u_ref[...], xt_ref[...])
        o_ref[...] += _dot(vt_ref[...], p_ref[...])

    pl.when(e % 2 == 0)(functools.partial(step, hid_a, hid_b))
    pl.when(e % 2 == 1)(functools.partial(step, hid_b, hid_a))


def _peer_dense(xtb, ub, vtb, tau, e1, s2, e2, *, tt, te):
    d, t = xtb.shape
    n_e = ub.shape[0] // te
    once = pl.Buffered(1)
    rspec = pl.BlockSpec((PEER_HEADS, PEER_NKEYS, tt), lambda i, e: (0, 0, i), pipeline_mode=once)
    est = (d * tt * 2 + 2 * te * d * 2 + 2 * d * te * 2 + 4 * PEER_HEADS * PEER_NKEYS * tt * 4
           + 2 * d * tt * 4 + te * tt * 2 + 2 * te * tt * 4 + 4 * te * tt * 4)
    return pl.pallas_call(
        _peer_dense_kernel,
        out_shape=jax.ShapeDtypeStruct((d, t), jnp.float32),
        grid=(t // tt, n_e + 1),
        in_specs=[pl.BlockSpec((d, tt), lambda i, e: (0, i), pipeline_mode=once),
                  pl.BlockSpec((te, d), lambda i, e: (jnp.minimum(e, n_e - 1), 0)),
                  pl.BlockSpec((d, te), lambda i, e: (0, jnp.maximum(e - 1, 0))),
                  rspec, rspec, rspec, rspec],
        out_specs=pl.BlockSpec((d, tt), lambda i, e: (0, i)),
        scratch_shapes=[pltpu.VMEM((te, tt), jnp.float32), pltpu.VMEM((te, tt), jnp.float32),
                        pltpu.VMEM((te, tt), jnp.bfloat16)],
        compiler_params=_cparams(("parallel", "arbitrary"), est),
        name="peer_dense",
    )(xtb, ub, vtb, tau, e1, s2, e2)


def _ffn_ln_kernel(tn, x_ref, ft_ref, g_ref, b_ref, of_ref, ob_ref):
    for c in range(of_ref.shape[1] // tn):
        sl = slice(c * tn, (c + 1) * tn)
        of_ref[:, sl] = DEEPNORM_ALPHA * x_ref[:, sl] + ft_ref[sl, :].T
    _ln_inplace(of_ref, ob_ref, g_ref, b_ref, tn)


def _ffn_ln(x, ffn_t, g, b, *, tm):
    t, d = x.shape
    row = pl.BlockSpec((tm, d), lambda i: (i, 0))
    vec = pl.BlockSpec((1, d), lambda i: (0, 0))
    return pl.pallas_call(
        functools.partial(_ffn_ln_kernel, 512),
        out_shape=[jax.ShapeDtypeStruct((t, d), jnp.float32), jax.ShapeDtypeStruct((t, d), jnp.bfloat16)],
        grid=(t // tm,),
        in_specs=[row, pl.BlockSpec((d, tm), lambda i: (0, i)), vec, vec],
        out_specs=[row, row],
        compiler_params=_cparams(("parallel",), 2 * (3 * tm * d * 4 + tm * d * 2) + 3 * tm * d * 4),
        name="ffn_ln",
    )(x, ffn_t, g, b)


def _col_tiles(sizes, modes, tn):
    out, lo = [], 0
    for size, mode in zip(sizes, modes):
        assert size % tn == 0
        out.append((lo // tn, (lo + size) // tn, mode))
        lo += size
    return out


def _even_mixer(xb, w_in, sink, tabs):
    q_scale = QK_SCALE * LOG2E
    rope_q, rope_k, plain = (0, ROPE_DIMS // 2, q_scale, None), (0, ROPE_DIMS // 2, 1.0, None), (None, 0, 1.0, None)
    modes = _col_tiles(AB_SIZES, [rope_q, rope_k, plain, rope_q, rope_k, plain], 512)
    gains = jnp.ones((8, HEAD_DIM), jnp.float32)
    qkv = _project(xb, w_in, tabs, gains, modes, tm=1024 if xb.shape[0] % 1024 == 0 else xb.shape[0], tn=512)
    cols = [0]
    for sz in AB_SIZES:
        cols.append(cols[-1] + sz // HEAD_DIM)
    oa = _window_attention(qkv, sink.reshape(-1), q_col=cols[0], k_col=cols[1], v_col=cols[2],
                           n_kv=A_KV_HEADS, group=A_GROUP, blk=A_WINDOW, window=A_WINDOW)
    gw = B_GROUP_HEADS * HEAD_DIM
    outs, lses = [], []
    for g, (win, dil) in enumerate(B_PAIRS):
        take = lambda c: qkv[:, (c * HEAD_DIM + g * gw):(c * HEAD_DIM + (g + 1) * gw)]
        o, l = _dilated_attention(take(cols[3]), take(cols[4]), take(cols[5]), dil=dil, blk=win // (2 * dil))
        outs.append(o)
        lses.append(l)
    ob = _merge_groups(outs, lses, tm=512)
    return jnp.concatenate([oa, ob], axis=-1)


def _odd_mixer(xb, w_in, lam, subln_g, q_norm_g, k_norm_g, tabs, lambda_init):
    q_scale = QK_SCALE * LOG2E
    half_ax = HEAD_DIM // 4
    rope_q, rope_k, plain = (0, ROPE_DIMS // 2, q_scale, None), (0, ROPE_DIMS // 2, 1.0, None), (None, 0, 1.0, None)
    ax_q, ax_k = (1, half_ax, q_scale, 0), (1, half_ax, 1.0, 1)
    modes = _col_tiles(CD_SIZES, [rope_q, rope_k, plain, ax_q, ax_k, plain], 512)
    gains = jnp.zeros((8, HEAD_DIM), jnp.float32).at[0].set(q_norm_g).at[1].set(k_norm_g)
    qkv = _project(xb, w_in, tabs, gains, modes, tm=1024 if xb.shape[0] % 1024 == 0 else xb.shape[0], tn=512)
    cols = [0]
    for sz in CD_SIZES:
        cols.append(cols[-1] + sz // HEAD_DIM)
    s = xb.shape[0]
    ts = min(512, s)
    vct = _transpose_cols(qkv, cols[2] * HEAD_DIM, CD_SIZES[2], ts=ts, tc=512)
    vdt = _transpose_cols(qkv, cols[5] * HEAD_DIM, CD_SIZES[5], ts=ts, tc=512)
    oc = _diff_attention(qkv, vct, lam, subln_g.reshape(1, -1), 1.0 - lambda_init, q_col=cols[0], k_col=cols[1],
                         heads=C_HEADS, tq=min(1024, s), tk=min(2048, s))
    od = _gqa_attention(qkv, vdt, q_col=cols[3], k_col=cols[4], n_kv=D_KV_HEADS, group=D_GROUP,
                        tq=min(256, s), tk=min(4096, s))
    return jnp.concatenate([oc, od], axis=-1)


def _peer_ffn(xf, xb, wq, keys, u, v, g, b):
    t = xb.shape[0]
    tt = min(512, t)
    tau, e1, s2, e2 = _peer_router(xb, wq.astype(jnp.bfloat16), keys.astype(jnp.bfloat16), tt=tt)
    ub = _cast_rows(u, jnp.bfloat16, ts=512)
    vtb = _transpose_cols(v, 0, v.shape[1], ts=1024, tc=1024, out_dtype=jnp.bfloat16)
    ffn_t = _peer_dense(xb.T, ub, vtb, tau, e1, s2, e2, tt=tt, te=512)
    return _ffn_ln(xf, ffn_t, g.reshape(1, -1), b.reshape(1, -1), tm=min(256, t))


def kernel(x, w_in_ab, sink_a, w_out_ab, w_in_cd, lam_q1, lam_k1, lam_q2, lam_k2, subln_g, q_norm_g, k_norm_g,
           w_out_cd, ln_mix_g, ln_mix_b, peer_wq, peer_keys, peer_u, peer_v, ln_ffn_g, ln_ffn_b):
    b, s, d = x.shape
    assert b == 1 and d == D_MODEL
    bf = jnp.bfloat16
    pos = jnp.arange(s)
    rows, cols = pos // GRID_W, pos % GRID_W
    tab_partial = _rope_tables([(0, ROPE_DIMS, pos, ROPE_THETA)])
    tab_axial = _rope_tables([(0, HEAD_DIM // 2, rows, AXIAL_THETA), (HEAD_DIM // 2, HEAD_DIM // 2, cols, AXIAL_THETA)])
    xf = x[0]
    xb = xf.astype(bf)
    for layer in range(DEPTH):
        i = layer // 2
        if layer % 2 == 0:
            mix_in = _even_mixer(xb, w_in_ab[i].astype(bf), sink_a[i], tab_partial)
            w_out = w_out_ab[i]
        else:
            lambda_init = 0.8 - 0.6 * math.exp(-0.3 * layer)
            f32 = jnp.float32
            lam = (jnp.exp(jnp.sum(lam_q1[i].astype(f32) * lam_k1[i].astype(f32)))
                   - jnp.exp(jnp.sum(lam_q2[i].astype(f32) * lam_k2[i].astype(f32))) + lambda_init)
            mix_in = _odd_mixer(xb, w_in_cd[i].astype(bf), lam.reshape(1), subln_g[i], q_norm_g[i], k_norm_g[i],
                                jnp.concatenate([tab_partial, tab_axial]), lambda_init)
            w_out = w_out_cd[i]
        xf, xb = _outproj_ln(mix_in, w_out.astype(bf), xf, ln_mix_g[layer].reshape(1, -1),
                             ln_mix_b[layer].reshape(1, -1), tm=min(512, s), tn=512)
        xf, xb = _peer_ffn(xf, xb, peer_wq[layer], peer_keys[layer], peer_u[layer], peer_v[layer],
                           ln_ffn_g[layer], ln_ffn_b[layer])
    return xf[None]
```
